```python
import jax, jax.numpy as jnp
from jax import lax
import numpy as np

D_MODEL = 1024
BATCH = 2
SEQ = 8192
DEPTH = 2
DEC_BATCH = 32
DEC_SEQ = 4
PAST_LEN = 8192
PAGE_SIZE = 128

MIX_GROUP = D_MODEL // 4
POOL_WINDOWS = (2, 4, 8, 16)
POOL_GROUPS = len(POOL_WINDOWS)
POOL_CH = MIX_GROUP // POOL_GROUPS
POOL_BUF = max(POOL_WINDOWS) - 1
SC_WIDTH = 3
SB_HEADS = 4
SB_HEAD_DIM = MIX_GROUP // SB_HEADS
CONF_WIDTH = 31
N_MEM = 256
MEM_HEADS = 4
MEM_HEAD_DIM = D_MODEL // MEM_HEADS
D_FF = 3584
N_EXPERTS = 8
TOP_K = 2
Q_BLOCK = 128
EPS = 1e-6
N_DENSE = (DEPTH + 1) // 2
N_MOE = DEPTH // 2
IN_COLS = 9 * MIX_GROUP

kernel_name = "hybrid_pool_conv_stickbreak_conformer_decoder_step"


def rmsnorm(x, g):
    xf = x.astype(jnp.float32)
    y = xf * lax.rsqrt(jnp.mean(xf * xf, axis=-1, keepdims=True) + EPS)
    return (y * g).astype(x.dtype)


def layernorm(x, g, b):
    xf = x.astype(jnp.float32)
    xc = xf - jnp.mean(xf, axis=-1, keepdims=True)
    y = xc * lax.rsqrt(jnp.mean(xc * xc, axis=-1, keepdims=True) + EPS)
    return (y * g + b).astype(x.dtype)


def causal_dwconv(u, buf, w):
    full = jnp.concatenate([buf.astype(u.dtype), u], axis=1)
    y = lax.conv_general_dilated(full, w[:, None, :].astype(u.dtype), (1,), "VALID",
                                 dimension_numbers=("NWC", "WIO", "NWC"),
                                 feature_group_count=u.shape[-1])
    return y, full[:, -(w.shape[0] - 1):]


def causal_pool(u, buf, pos, w_grp, scale):
    b, t, _ = u.shape
    full = jnp.concatenate([buf.astype(u.dtype), u], axis=1)
    cs = jnp.cumsum(full.astype(jnp.float32), axis=1)
    cs = jnp.concatenate([jnp.zeros_like(cs[:, :1]), cs], axis=1)
    cs = cs.reshape(b, t + POOL_BUF + 1, POOL_GROUPS, POOL_CH)
    end = cs[:, POOL_BUF + 1:]
    sums = []
    for g, w in enumerate(POOL_WINDOWS):
        start = POOL_BUF + 1 - w
        sums.append(end[:, :, g] - cs[:, start:start + t, g])
    win = jnp.stack(sums, axis=2)
    cnt = jnp.minimum(pos[:, None] + 1, jnp.array(POOL_WINDOWS, jnp.int32)).astype(jnp.float32)
    p = win / cnt[None, :, :, None] - u.reshape(b, t, POOL_GROUPS, POOL_CH).astype(jnp.float32)
    z = jnp.einsum("btgc,gcd->btgd", p.astype(u.dtype), w_grp).reshape(b, t, MIX_GROUP)
    return z * scale, full[:, -POOL_BUF:]


def stick_breaking(q, k, v, q_pos, k_pos, bias):
    z = jnp.einsum("bqhd,bkhd->bhqk", q, k, preferred_element_type=jnp.float32) * (SB_HEAD_DIM ** -0.5)
    z = z + bias.astype(jnp.float32)[None, :, None, None]
    mask = k_pos[None, :] < q_pos[:, None]
    log_keep = jnp.where(mask, jax.nn.log_sigmoid(-z), 0.0)
    log_rest = lax.cumsum(log_keep, axis=3, reverse=True) - log_keep
    a = jnp.where(mask, jnp.exp(jax.nn.log_sigmoid(z) + log_rest), 0.0)
    return jnp.einsum("bhqk,bkhd->bqhd", a.astype(v.dtype), v)


def sb_prompt(q, k, v, bias):
    b, t = q.shape[:2]
    nb = t // Q_BLOCK
    qb = jnp.moveaxis(q.reshape(b, nb, Q_BLOCK, SB_HEADS, SB_HEAD_DIM), 1, 0)
    k_pos = jnp.arange(t)

    def block(args):
        q_blk, start = args
        return stick_breaking(q_blk, k, v, start + jnp.arange(Q_BLOCK), k_pos, bias)

    o = lax.map(block, (qb, jnp.arange(nb) * Q_BLOCK))
    return jnp.moveaxis(o, 0, 1).reshape(b, t, SB_HEADS, SB_HEAD_DIM)


def mem_project(mem, g_mem, w_mk, w_mv):
    b, n, _ = mem.shape
    m = rmsnorm(mem, g_mem)
    mk = (m @ w_mk).reshape(b, n, MEM_HEADS, MEM_HEAD_DIM)
    mv = (m @ w_mv).reshape(b, n, MEM_HEADS, MEM_HEAD_DIM)
    return mk, mv


def cross_attend(h, mk, mv, w_mq, w_mo):
    b, t, _ = h.shape
    q = (h @ w_mq).reshape(b, t, MEM_HEADS, MEM_HEAD_DIM)
    s = jnp.einsum("bthd,bmhd->bhtm", q, mk, preferred_element_type=jnp.float32) * (MEM_HEAD_DIM ** -0.5)
    pr = jax.nn.softmax(s, axis=-1).astype(mv.dtype)
    o = jnp.einsum("bhtm,bmhd->bthd", pr, mv).reshape(b, t, D_MODEL)
    return o @ w_mo


def swiglu(h, wg, wu, wd):
    return (jax.nn.silu(h @ wg) * (h @ wu)) @ wd


def moe_swiglu(h, w_router, wg, wu, wd):
    logits = jnp.einsum("btd,de->bte", h, w_router, preferred_element_type=jnp.float32)
    top_v, top_i = lax.top_k(logits, TOP_K)
    gates = jax.nn.softmax(top_v, axis=-1)
    combine = jnp.einsum("btk,btke->bte", gates,
                         jax.nn.one_hot(top_i, N_EXPERTS, dtype=jnp.float32)).astype(h.dtype)
    out = jnp.zeros_like(h)
    for e in range(N_EXPERTS):
        out = out + combine[..., e:e + 1] * swiglu(h, wg[e], wu[e], wd[e])
    return out


def decoder_layer(x, pos, i, attend, buf_pool, buf_sc, buf_conf, mk, mv, p):
    b, t, _ = x.shape
    h = rmsnorm(x, p["g_mix"][i])
    u_a, h_b, c_b, b_b, q, k, v, a_d, g_d = jnp.split(h @ p["w_in"][i], 9, axis=-1)
    y_a, nb_pool = causal_pool(u_a, buf_pool, pos, p["w_pool"][i], p["pool_scale"][i])
    conv_b, nb_sc = causal_dwconv(c_b * h_b, buf_sc, p["w_sconv"][i])
    y_b = b_b * conv_b
    k = k.reshape(b, t, SB_HEADS, SB_HEAD_DIM)
    v = v.reshape(b, t, SB_HEADS, SB_HEAD_DIM)
    y_c = attend(q.reshape(b, t, SB_HEADS, SB_HEAD_DIM), k, v, p["sb_bias"][i]).reshape(b, t, MIX_GROUP)
    conv_d, nb_conf = causal_dwconv(a_d * jax.nn.sigmoid(g_d), buf_conf, p["w_conf"][i])
    y_d = jax.nn.silu(layernorm(conv_d + p["b_conf"][i], p["g_conf_ln"][i], p["b_conf_ln"][i]))
    x = x + jnp.concatenate([y_a, y_b, y_c, y_d], axis=-1) @ p["w_out"][i]
    x = x + cross_attend(rmsnorm(x, p["g_cross"][i]), mk, mv, p["w_mq"][i], p["w_mo"][i])
    h = rmsnorm(x, p["g_ffn"][i])
    j = i // 2
    if i % 2 == 0:
        x = x + swiglu(h, p["w_ff_gate"][j], p["w_ff_up"][j], p["w_ff_down"][j])
    else:
        x = x + moe_swiglu(h, p["w_router"][j], p["w_moe_gate"][j], p["w_moe_up"][j], p["w_moe_down"][j])
    return x, k, v, nb_pool, nb_sc, nb_conf


def setup_inputs(seed: int = 0) -> dict:
    key = jax.random.key(seed)
    ks = iter(jax.random.split(key, 48))

    def nrm(shape, scale=1.0):
        return jax.random.normal(next(ks), shape, jnp.float32) * scale

    def gain(shape):
        return 1.0 + nrm(shape, 0.02)

    n_pages = PAST_LEN // PAGE_SIZE
    n_used = DEC_BATCH * n_pages
    n_pool = n_used + (n_used + 3) // 4
    page_table = jax.random.permutation(next(ks), n_pool)[:n_used].reshape(DEC_BATCH, n_pages).astype(jnp.int32)
    D = D_MODEL
    return {
        "x_prompt": nrm((BATCH, SEQ, D)),
        "x_sample": nrm((DEC_BATCH, DEC_SEQ, D)),
        "cache_k": nrm((DEPTH, n_pool, PAGE_SIZE, SB_HEADS, SB_HEAD_DIM)),
        "cache_v": nrm((DEPTH, n_pool, PAGE_SIZE, SB_HEADS, SB_HEAD_DIM)),
        "cache_mem_k": nrm((DEPTH, DEC_BATCH, N_MEM, MEM_HEADS, MEM_HEAD_DIM)),
        "cache_mem_v": nrm((DEPTH, DEC_BATCH, N_MEM, MEM_HEADS, MEM_HEAD_DIM)),
        "state_pool": nrm((DEPTH, DEC_BATCH, POOL_BUF, MIX_GROUP)),
        "state_sconv": nrm((DEPTH, DEC_BATCH, SC_WIDTH - 1, MIX_GROUP)),
        "state_conf": nrm((DEPTH, DEC_BATCH, CONF_WIDTH - 1, MIX_GROUP)),
        "page_table": page_table,
        "mem_prompt": nrm((BATCH, N_MEM, D)),
        "g_mix": gain((DEPTH, D)),
        "w_in": nrm((DEPTH, D, IN_COLS), D ** -0.5),
        "w_pool": nrm((DEPTH, POOL_GROUPS, POOL_CH, POOL_CH), POOL_CH ** -0.5),
        "pool_scale": gain((DEPTH, MIX_GROUP)),
        "w_sconv": nrm((DEPTH, SC_WIDTH, MIX_GROUP), SC_WIDTH ** -0.5),
        "sb_bias": -6.0 - 3.0 * jax.random.uniform(next(ks), (DEPTH, SB_HEADS), jnp.float32),
        "w_conf": nrm((DEPTH, CONF_WIDTH, MIX_GROUP), CONF_WIDTH ** -0.5),
        "b_conf": nrm((DEPTH, MIX_GROUP), 0.02),
        "g_conf_ln": gain((DEPTH, MIX_GROUP)),
        "b_conf_ln": nrm((DEPTH, MIX_GROUP), 0.02),
        "w_out": nrm((DEPTH, D, D), D ** -0.5),
        "g_cross": gain((DEPTH, D)),
        "g_mem": gain((DEPTH, D)),
        "w_mq": nrm((DEPTH, D, D), D ** -0.5),
        "w_mk": nrm((DEPTH, D, D), D ** -0.5),
        "w_mv": nrm((DEPTH, D, D), D ** -0.5),
        "w_mo": nrm((DEPTH, D, D), D ** -0.5),
        "g_ffn": gain((DEPTH, D)),
        "w_ff_gate": nrm((N_DENSE, D, D_FF), D ** -0.5),
        "w_ff_up": nrm((N_DENSE, D, D_FF), D ** -0.5),
        "w_ff_down": nrm((N_DENSE, D_FF, D), D_FF ** -0.5),
        "w_router": nrm((N_MOE, D, N_EXPERTS), D ** -0.5),
        "w_moe_gate": nrm((N_MOE, N_EXPERTS, D, D_FF), D ** -0.5),
        "w_moe_up": nrm((N_MOE, N_EXPERTS, D, D_FF), D ** -0.5),
        "w_moe_down": nrm((N_MOE, N_EXPERTS, D_FF, D), D_FF ** -0.5),
        "g_final": gain((D,)),
    }


def reference(x_prompt, x_sample, cache_k, cache_v, cache_mem_k, cache_mem_v, state_pool, state_sconv,
              state_conf, page_table, mem_prompt, g_mix, w_in, w_pool, pool_scale, w_sconv, sb_bias, w_conf,
              b_conf, g_conf_ln, b_conf_ln, w_out, g_cross, g_mem, w_mq, w_mk, w_mv, w_mo, g_ffn, w_ff_gate,
              w_ff_up, w_ff_down, w_router, w_moe_gate, w_moe_up, w_moe_down, g_final):
    p = {"g_mix": g_mix, "w_in": w_in, "w_pool": w_pool, "pool_scale": pool_scale, "w_sconv": w_sconv,
         "sb_bias": sb_bias, "w_conf": w_conf, "b_conf": b_conf, "g_conf_ln": g_conf_ln,
         "b_conf_ln": b_conf_ln, "w_out": w_out, "g_cross": g_cross, "w_mq": w_mq, "w_mo": w_mo,
         "g_ffn": g_ffn, "w_ff_gate": w_ff_gate, "w_ff_up": w_ff_up, "w_ff_down": w_ff_down,
         "w_router": w_router, "w_moe_gate": w_moe_gate, "w_moe_up": w_moe_up, "w_moe_down": w_moe_down}
    bp, tp = x_prompt.shape[:2]
    bs, ts = x_sample.shape[:2]
    past = page_table.shape[1] * PAGE_SIZE
    pos_p = jnp.arange(tp)
    pos_s = past + jnp.arange(ts)
    k_pos_s = jnp.arange(past + ts)
    xp, xs = x_prompt, x_sample
    kp_l, vp_l, ks_l, vs_l, mkp_l, mvp_l = [], [], [], [], [], []
    poolp_l, pools_l, scp_l, scs_l, cfp_l, cfs_l = [], [], [], [], [], []
    for i in range(DEPTH):
        mk_p, mv_p = mem_project(mem_prompt, g_mem[i], w_mk[i], w_mv[i])
        zp = jnp.zeros((bp, CONF_WIDTH - 1, MIX_GROUP), xp.dtype)
        xp, k, v, b_pool, b_sc, b_cf = decoder_layer(
            xp, pos_p, i, sb_prompt, zp[:, :POOL_BUF], zp[:, :SC_WIDTH - 1], zp, mk_p, mv_p, p)
        kp_l.append(k); vp_l.append(v); mkp_l.append(mk_p); mvp_l.append(mv_p)
        poolp_l.append(b_pool); scp_l.append(b_sc); cfp_l.append(b_cf)

        kc = cache_k[i][page_table].reshape(bs, past, SB_HEADS, SB_HEAD_DIM)
        vc = cache_v[i][page_table].reshape(bs, past, SB_HEADS, SB_HEAD_DIM)

        def sb_sample(q, k_new, v_new, bias, kc=kc, vc=vc):
            k_all = jnp.concatenate([kc.astype(k_new.dtype), k_new], axis=1)
            v_all = jnp.concatenate([vc.astype(v_new.dtype), v_new], axis=1)
            return stick_breaking(q, k_all, v_all, pos_s, k_pos_s, bias)

        xs, k, v, b_pool, b_sc, b_cf = decoder_layer(
            xs, pos_s, i, sb_sample, state_pool[i], state_sconv[i], state_conf[i],
            cache_mem_k[i], cache_mem_v[i], p)
        ks_l.append(k); vs_l.append(v)
        pools_l.append(b_pool); scs_l.append(b_sc); cfs_l.append(b_cf)

    y_prompt = rmsnorm(xp, g_final)
    y_sample = rmsnorm(xs, g_final)
    return (y_prompt, y_sample,
            jnp.stack(kp_l), jnp.stack(vp_l), jnp.stack(ks_l), jnp.stack(vs_l),
            jnp.stack(mkp_l), jnp.stack(mvp_l),
            jnp.stack(poolp_l), jnp.stack(pools_l), jnp.stack(scp_l), jnp.stack(scs_l),
            jnp.stack(cfp_l), jnp.stack(cfs_l))
```

```python
import functools

import jax
import jax.numpy as jnp
from jax import lax
from jax.experimental import pallas as pl
from jax.experimental.pallas import tpu as pltpu

EPS = 1e-6
SB_HEADS = 4
MEM_HEADS = 4
POOL_WINDOWS = (2, 4, 8, 16)
TOP_K = 2
LANES = 128
SUBLANES = 8
VMEM_LIMIT_BYTES = 56 * 1024 * 1024
BF16 = jnp.bfloat16
F32 = jnp.float32


def _params(*sem):
    return pltpu.CompilerParams(dimension_semantics=sem, vmem_limit_bytes=VMEM_LIMIT_BYTES)


def _rms(x, g):
    return x * lax.rsqrt(jnp.mean(x * x, axis=-1, keepdims=True) + EPS) * g


def _dot(a, b):
    return jnp.dot(a, b, preferred_element_type=F32)


def _dot_nt(a, b):
    return lax.dot_general(a, b, (((1,), (1,)), ((), ())), preferred_element_type=F32)


def _row_tile(m, want):
    if m <= want:
        return m
    for t in range(want - want % SUBLANES, 0, -SUBLANES):
        if m % t == 0:
            return t
    raise ValueError(f"no aligned row tile for {m}")


def _in_proj_kernel(x_ref, g_ref, w_ref, pa_ref, pb_ref, q_ref, k_ref, v_ref, kb_ref, vb_ref, pd_ref, *, c, qscale):
    h = _rms(x_ref[...], g_ref[...]).astype(BF16)
    pa_ref[...] = _dot(h, w_ref[:, 0:c])
    pb_ref[...] = _dot(h, w_ref[:, c:4 * c])
    q_ref[...] = (_dot(h, w_ref[:, 4 * c:5 * c]) * qscale).astype(BF16)
    k = _dot(h, w_ref[:, 5 * c:6 * c])
    v = _dot(h, w_ref[:, 6 * c:7 * c])
    k_ref[...] = k
    v_ref[...] = v
    kb_ref[...] = k.astype(BF16)
    vb_ref[...] = v.astype(BF16)
    pd_ref[...] = _dot(h, w_ref[:, 7 * c:9 * c])


def in_proj(x, g, w_bf16, *, qscale, tm=512):
    m, d = x.shape
    c = w_bf16.shape[1] // 9
    tm = _row_tile(m, tm)
    widths = (c, 3 * c, c, c, c, c, c, 2 * c)
    dtypes = (F32, F32, BF16, F32, F32, BF16, BF16, F32)
    return pl.pallas_call(
        functools.partial(_in_proj_kernel, c=c, qscale=qscale),
        grid=(m // tm,),
        in_specs=[pl.BlockSpec((tm, d), lambda i: (i, 0)),
                  pl.BlockSpec((1, d), lambda i: (0, 0)),
                  pl.BlockSpec((d, 9 * c), lambda i: (0, 0))],
        out_specs=[pl.BlockSpec((tm, wd), lambda i: (i, 0)) for wd in widths],
        out_shape=[jax.ShapeDtypeStruct((m, wd), dt) for wd, dt in zip(widths, dtypes)],
        compiler_params=_params("parallel"),
        name="in_proj",
    )(x, g.reshape(1, d), w_bf16)


POOL_HALO = 16
SCONV_HALO = 8
CONF_HALO = 32


def _stage(buf_ref, state_ref, new_rows, halo, tt, first):
    @pl.when(first)
    def _():
        buf_ref[0:halo, :] = state_ref[0]
    buf_ref[halo:halo + tt, :] = new_rows


def _carry_history(buf_ref, halo, tt):
    buf_ref[0:halo, :] = buf_ref[tt:tt + halo, :]


def _causal_conv(buf_ref, w_ref, halo, tt, width):
    acc = None
    for j in range(width):
        off = halo - (width - 1) + j
        term = buf_ref[off:off + tt, :] * w_ref[j:j + 1, :]
        acc = term if acc is None else acc + term
    return acc


def _mixers_kernel(pa_ref, pb_ref, pd_ref, spool_ref, ssc_ref, scf_ref,
                   wpool_ref, pscale_ref, wsc_ref, wcf_ref, bcf_ref, gln_ref, bln_ref,
                   ya_ref, yb_ref, yd_ref, npool_ref, nsc_ref, ncf_ref,
                   bufa_ref, bufb_ref, bufd_ref, *, tt, n_tiles, t_valid, pos0, sc_width, cf_width):
    i = pl.program_id(1)
    first = i == 0
    c = ya_ref.shape[-1]

    u = pa_ref[0]
    _stage(bufa_ref, spool_ref, u, POOL_HALO, tt, first)
    sums = {}
    acc = u
    for j in range(1, max(POOL_WINDOWS)):
        acc = acc + bufa_ref[POOL_HALO - j:POOL_HALO - j + tt, :]
        if j + 1 in POOL_WINDOWS:
            sums[j + 1] = acc
    lane = lax.broadcasted_iota(jnp.int32, (tt, c), 1)
    grp = lane // (c // len(POOL_WINDOWS))
    win = sums[POOL_WINDOWS[-1]]
    wlen = jnp.full((tt, c), POOL_WINDOWS[-1], jnp.int32)
    for gi in range(len(POOL_WINDOWS) - 2, -1, -1):
        win = jnp.where(grp == gi, sums[POOL_WINDOWS[gi]], win)
        wlen = jnp.where(grp == gi, POOL_WINDOWS[gi], wlen)
    pos = pos0 + i * tt + lax.broadcasted_iota(jnp.int32, (tt, c), 0)
    cnt = jnp.minimum(pos + 1, wlen).astype(F32)
    p = win / cnt - u
    ya_ref[0] = _dot(p.astype(BF16), wpool_ref[...]) * pscale_ref[...]

    pb = pb_ref[0]
    hb, cb, bb = pb[:, 0:c], pb[:, c:2 * c], pb[:, 2 * c:3 * c]
    _stage(bufb_ref, ssc_ref, cb * hb, SCONV_HALO, tt, first)
    yb_ref[0] = bb * _causal_conv(bufb_ref, wsc_ref, SCONV_HALO, tt, sc_width)

    pd = pd_ref[0]
    ad, gd = pd[:, 0:c], pd[:, c:2 * c]
    _stage(bufd_ref, scf_ref, ad * jax.nn.sigmoid(gd), CONF_HALO, tt, first)
    cv = _causal_conv(bufd_ref, wcf_ref, CONF_HALO, tt, cf_width) + bcf_ref[...]
    xc = cv - jnp.mean(cv, axis=-1, keepdims=True)
    ln = xc * lax.rsqrt(jnp.mean(xc * xc, axis=-1, keepdims=True) + EPS) * gln_ref[...] + bln_ref[...]
    yd_ref[0] = ln * jax.nn.sigmoid(ln)

    @pl.when(i == n_tiles - 1)
    def _():
        for out_ref, buf_ref, halo, keep in ((npool_ref, bufa_ref, POOL_HALO, max(POOL_WINDOWS) - 1),
                                             (nsc_ref, bufb_ref, SCONV_HALO, sc_width - 1),
                                             (ncf_ref, bufd_ref, CONF_HALO, cf_width - 1)):
            end = halo + t_valid
            out_ref[0] = buf_ref[end - keep:end, :]

    if n_tiles > 1:
        _carry_history(bufa_ref, POOL_HALO, tt)
        _carry_history(bufb_ref, SCONV_HALO, tt)
        _carry_history(bufd_ref, CONF_HALO, tt)


def _pad_state(state, halo):
    return jnp.pad(state, ((0, 0), (halo - state.shape[1], 0), (0, 0)))


def local_mixers(pa, pb, pd, st_pool, st_sc, st_cf, wpool_bd, pool_scale, w_sc, w_cf, b_cf, g_ln, b_ln,
                 *, t_valid, pos0, tt=512):
    b, t, c = pa.shape
    tt = _row_tile(t, tt)
    n_tiles = t // tt
    assert n_tiles == 1 or tt >= CONF_HALO
    sc_width, cf_width = w_sc.shape[0], w_cf.shape[0]
    keep_pool = max(POOL_WINDOWS) - 1
    tile = lambda wd: pl.BlockSpec((1, tt, wd), lambda bi, i: (bi, i, 0))
    per_b = lambda r, wd: pl.BlockSpec((1, r, wd), lambda bi, i: (bi, 0, 0))
    full2 = lambda a: pl.BlockSpec(a.shape, lambda bi, i: (0, 0))
    row = lambda a: a.reshape(1, c)
    weights = (wpool_bd, row(pool_scale), w_sc, w_cf, row(b_cf), row(g_ln), row(b_ln))
    last_valid = t_valid - (n_tiles - 1) * tt
    return pl.pallas_call(
        functools.partial(_mixers_kernel, tt=tt, n_tiles=n_tiles, t_valid=last_valid, pos0=pos0,
                          sc_width=sc_width, cf_width=cf_width),
        grid=(b, n_tiles),
        in_specs=[tile(c), tile(3 * c), tile(2 * c),
                  per_b(POOL_HALO, c), per_b(SCONV_HALO, c), per_b(CONF_HALO, c)] + [full2(a) for a in weights],
        out_specs=[tile(c), tile(c), tile(c),
                   per_b(keep_pool, c), per_b(sc_width - 1, c), per_b(cf_width - 1, c)],
        out_shape=[jax.ShapeDtypeStruct((b, t, c), F32)] * 3 + [
            jax.ShapeDtypeStruct((b, keep_pool, c), F32),
            jax.ShapeDtypeStruct((b, sc_width - 1, c), F32),
            jax.ShapeDtypeStruct((b, cf_width - 1, c), F32)],
        scratch_shapes=[pltpu.VMEM((POOL_HALO + tt, c), F32),
                        pltpu.VMEM((SCONV_HALO + tt, c), F32),
                        pltpu.VMEM((CONF_HALO + tt, c), F32)],
        compiler_params=_params("parallel", "arbitrary"),
        name="local_mixers",
    )(pa, pb, pd, _pad_state(st_pool, POOL_HALO), _pad_state(st_sc, SCONV_HALO), _pad_state(st_cf, CONF_HALO),
      *weights)


def _suffix_matrix(tk):
    r = lax.broadcasted_iota(jnp.int32, (2 * tk, tk), 0) % tk
    s = lax.broadcasted_iota(jnp.int32, (2 * tk, tk), 1)
    return jnp.where(r > s, 1.0, 0.0).astype(BF16)


def _sb_weights(z, carry, mask, suffix):
    sp = jnp.maximum(z, 0.0) + jnp.log1p(jnp.exp(-jnp.abs(z)))
    if mask is not None:
        sp = jnp.where(mask, sp, 0.0)
    hi = sp.astype(BF16)
    lo = (sp - hi.astype(F32)).astype(BF16)
    later = _dot(jnp.concatenate([hi, lo], axis=1), suffix)
    a = jnp.exp(z - sp - later - carry)
    if mask is not None:
        a = jnp.where(mask, a, 0.0)
    return a.astype(BF16), carry + jnp.sum(sp, axis=1, keepdims=True)


def _head_select(x, c, heads):
    lane_head = lax.broadcasted_iota(jnp.int32, x.shape, x.ndim - 1) // (c // heads)
    return [jnp.where(lane_head == h, x, jnp.zeros_like(x)) for h in range(heads)]


def _sb_prompt_kernel(bias_ref, q_ref, k_ref, v_ref, o_ref, acc_ref, *, tq):
    qi = pl.program_id(1)
    c = q_ref.shape[-1]
    qh = _head_select(q_ref[0], c, SB_HEADS)
    suffix = _suffix_matrix(tq)
    row = lax.broadcasted_iota(jnp.int32, (tq, tq), 0)
    col = lax.broadcasted_iota(jnp.int32, (tq, tq), 1)
    diag_mask = col < row

    def key_block(kbi, carries, mask):
        start = pl.multiple_of(kbi * tq, tq)
        kb = k_ref[0, pl.ds(start, tq), :]
        vb = v_ref[0, pl.ds(start, tq), :]
        new = []
        for h in range(SB_HEADS):
            a, cr = _sb_weights(_dot_nt(qh[h], kb) + bias_ref[h], carries[h], mask, suffix)
            o = _dot(a, vb)
            if mask is not None:
                acc_ref[h] = o
            else:
                acc_ref[h] += o
            new.append(cr)
        return tuple(new)

    zero = jnp.zeros((tq, 1), F32)
    carries = key_block(qi, (zero,) * SB_HEADS, diag_mask)
    lax.fori_loop(0, qi, lambda j, cs: key_block(qi - 1 - j, cs, None), carries)
    parts = [_head_select(acc_ref[h], c, SB_HEADS)[h] for h in range(SB_HEADS)]
    o_ref[0] = functools.reduce(lambda x, y: x + y, parts)


def sb_prompt(q, k, v, bias, *, tq=256):
    b, t, c = q.shape
    tq = _row_tile(t, tq)
    return pl.pallas_call(
        functools.partial(_sb_prompt_kernel, tq=tq),
        grid_spec=pltpu.PrefetchScalarGridSpec(
            num_scalar_prefetch=1,
            grid=(b, t // tq),
            in_specs=[pl.BlockSpec((1, tq, c), lambda bi, i, *_: (bi, i, 0)),
                      pl.BlockSpec((1, t, c), lambda bi, i, *_: (bi, 0, 0)),
                      pl.BlockSpec((1, t, c), lambda bi, i, *_: (bi, 0, 0))],
            out_specs=pl.BlockSpec((1, tq, c), lambda bi, i, *_: (bi, i, 0)),
            scratch_shapes=[pltpu.VMEM((SB_HEADS, tq, c), F32)]),
        out_shape=jax.ShapeDtypeStruct((b, t, c), F32),
        compiler_params=_params("parallel", "arbitrary"),
        name="sb_prompt",
    )(bias, q, k, v)


SAMPLE_ROWS = 16
PAGES_PER_STEP = 8


def _sb_sample_kernel(pt_ref, bias_ref, q_ref, kn_ref, vn_ref, *rest, n_pages_step):
    k_refs = rest[:n_pages_step]
    v_refs = rest[n_pages_step:2 * n_pages_step]
    o_ref, acc_ref, carry_ref = rest[2 * n_pages_step:]
    j = pl.program_id(1)
    page = kn_ref.shape[-1]
    rows = SB_HEADS * SAMPLE_ROWS
    q = [q_ref[0, h].astype(BF16) for h in range(SB_HEADS)]
    row = lax.broadcasted_iota(jnp.int32, (rows, 1), 0)
    bias = jnp.zeros((rows, 1), F32)
    for h in range(SB_HEADS):
        bias = jnp.where(row // SAMPLE_ROWS == h, bias_ref[h], bias)
    suffix = _suffix_matrix(page)

    def step(kt_ref, vt_ref, mask):
        z = jnp.concatenate([_dot(q[h], kt_ref[h].astype(BF16)) for h in range(SB_HEADS)], axis=0) + bias
        a, cr = _sb_weights(z, carry_ref[...], mask, suffix)
        carry_ref[...] = cr
        for h in range(SB_HEADS):
            acc_ref[h] += _dot_nt(a[h * SAMPLE_ROWS:(h + 1) * SAMPLE_ROWS], vt_ref[h].astype(BF16))

    @pl.when(j == 0)
    def _():
        acc_ref[...] = jnp.zeros_like(acc_ref)
        carry_ref[...] = jnp.zeros_like(carry_ref)
        t_new = lax.broadcasted_iota(jnp.int32, (rows, page), 0) % SAMPLE_ROWS
        s_new = lax.broadcasted_iota(jnp.int32, (rows, page), 1)
        step(kn_ref.at[0], vn_ref.at[0], s_new < t_new)

    for p in range(n_pages_step):
        step(k_refs[p], v_refs[p], None)

    @pl.when(j == pl.num_programs(1) - 1)
    def _():
        o_ref[0] = acc_ref[...]


def sb_sample(q, kt_new, vt_new, cache_kt, cache_vt, layer, page_table, bias):
    b, heads, rows_t, dh = q.shape
    assert heads == SB_HEADS and rows_t == SAMPLE_ROWS
    page = cache_kt.shape[-1]
    n_pages = page_table.shape[1]
    pps = PAGES_PER_STEP if n_pages % PAGES_PER_STEP == 0 else 1
    n_steps = n_pages // pps

    def page_spec(p):
        return pl.BlockSpec((None, None, heads, dh, page),
                            lambda bi, j, pt, bs: (layer, pt[bi, n_pages - 1 - (j * pps + p)], 0, 0, 0))

    per_b = lambda r, w: pl.BlockSpec((1, heads, r, w), lambda bi, j, *_: (bi, 0, 0, 0))
    return pl.pallas_call(
        functools.partial(_sb_sample_kernel, n_pages_step=pps),
        grid_spec=pltpu.PrefetchScalarGridSpec(
            num_scalar_prefetch=2,
            grid=(b, n_steps),
            in_specs=[per_b(SAMPLE_ROWS, dh), per_b(dh, page), per_b(dh, page)]
                     + [page_spec(p) for p in range(pps)] + [page_spec(p) for p in range(pps)],
            out_specs=per_b(SAMPLE_ROWS, dh),
            scratch_shapes=[pltpu.VMEM((heads, SAMPLE_ROWS, dh), F32),
                            pltpu.VMEM((heads * SAMPLE_ROWS, 1), F32)]),
        out_shape=jax.ShapeDtypeStruct((b, heads, SAMPLE_ROWS, dh), F32),
        compiler_params=_params("parallel", "arbitrary"),
        name="sb_sample",
    )(page_table, bias, q, kt_new, vt_new, *([cache_kt] * pps), *([cache_vt] * pps))


def _out_proj_kernel(x_ref, ya_ref, yb_ref, yc_ref, yd_ref, w_ref, o_ref):
    y = jnp.concatenate([r[...].astype(BF16) for r in (ya_ref, yb_ref, yc_ref, yd_ref)], axis=1)
    o_ref[...] = x_ref[...] + _dot(y, w_ref[...])


def out_proj(x, ya, yb, yc, yd, w_bf16, *, tm=512):
    m, d = x.shape
    c = ya.shape[1]
    tm = _row_tile(m, tm)
    return pl.pallas_call(
        _out_proj_kernel,
        grid=(m // tm,),
        in_specs=[pl.BlockSpec((tm, d), lambda i: (i, 0))] + [pl.BlockSpec((tm, c), lambda i: (i, 0))] * 4
                 + [pl.BlockSpec((d, d), lambda i: (0, 0))],
        out_specs=pl.BlockSpec((tm, d), lambda i: (i, 0)),
        out_shape=jax.ShapeDtypeStruct((m, d), F32),
        compiler_params=_params("parallel"),
        name="out_proj",
    )(x, ya, yb, yc, yd, w_bf16)


def _norm_matmul_kernel(x_ref, g_ref, w_ref, o_ref):
    o_ref[...] = _dot(_rms(x_ref[...], g_ref[...]).astype(BF16), w_ref[...])


def norm_matmul(x, g, w_bf16, *, tm=512):
    m, d = x.shape
    n = w_bf16.shape[1]
    tm = _row_tile(m, tm)
    return pl.pallas_call(
        _norm_matmul_kernel,
        grid=(m // tm,),
        in_specs=[pl.BlockSpec((tm, d), lambda i: (i, 0)),
                  pl.BlockSpec((1, d), lambda i: (0, 0)),
                  pl.BlockSpec((d, n), lambda i: (0, 0))],
        out_specs=pl.BlockSpec((tm, n), lambda i: (i, 0)),
        out_shape=jax.ShapeDtypeStruct((m, n), F32),
        compiler_params=_params("parallel"),
        name="norm_matmul",
    )(x, g.reshape(1, d), w_bf16)


def _cross_kernel(x_ref, g_ref, wq_ref, mk_ref, mv_ref, wo_ref, o_ref, *, scale):
    x = x_ref[0]
    d = x.shape[-1]
    dh = d // MEM_HEADS
    q = (_dot(_rms(x, g_ref[...]).astype(BF16), wq_ref[...]) * scale).astype(BF16)
    outs = []
    for h in range(MEM_HEADS):
        s = _dot_nt(q[:, h * dh:(h + 1) * dh], mk_ref[0, :, h, :].astype(BF16))
        e = jnp.exp(s - jnp.max(s, axis=-1, keepdims=True))
        pr = e / jnp.sum(e, axis=-1, keepdims=True)
        outs.append(_dot(pr.astype(BF16), mv_ref[0, :, h, :].astype(BF16)).astype(BF16))
    o_ref[0] = x + _dot(jnp.concatenate(outs, axis=1), wo_ref[...])


def cross_block(x, g, wq_bf16, mk, mv, wo_bf16, *, tm=512):
    b, t, d = x.shape
    _, n_mem, heads, dh = mk.shape
    assert heads == MEM_HEADS and heads * dh == d
    tm = _row_tile(t, tm)
    scale = float(dh ** -0.5)
    const = lambda a: pl.BlockSpec(a.shape, lambda bi, i: (0,) * a.ndim)
    g2 = g.reshape(1, d)
    mem_spec = pl.BlockSpec((1, n_mem, heads, dh), lambda bi, i: (bi, 0, 0, 0))
    return pl.pallas_call(
        functools.partial(_cross_kernel, scale=scale),
        grid=(b, t // tm),
        in_specs=[pl.BlockSpec((1, tm, d), lambda bi, i: (bi, i, 0)), const(g2), const(wq_bf16),
                  mem_spec, mem_spec, const(wo_bf16)],
        out_specs=pl.BlockSpec((1, tm, d), lambda bi, i: (bi, i, 0)),
        out_shape=jax.ShapeDtypeStruct((b, t, d), F32),
        compiler_params=_params("parallel", "parallel"),
        name="cross_block",
    )(x, g2, wq_bf16, mk, mv, wo_bf16)


def _ffn_kernel(te_ref, tv_ref, x_ref, g_ref, wg_ref, wu_ref, wd_ref, o_ref, h_ref, acc_ref, *, normed_input):
    i, j = pl.program_id(0), pl.program_id(1)
    valid = tv_ref[i] > 0

    @pl.when(jnp.logical_and(valid, j == 0))
    def _():
        if normed_input:
            h_ref[...] = x_ref[...]
        else:
            h_ref[...] = _rms(x_ref[...], g_ref[...]).astype(BF16)
        acc_ref[...] = jnp.zeros_like(acc_ref)

    @pl.when(valid)
    def _():
        h = h_ref[...]
        gate = _dot(h, wg_ref[0])
        up = _dot(h, wu_ref[0])
        act = gate * jax.nn.sigmoid(gate) * up
        acc_ref[...] += _dot(act.astype(BF16), wd_ref[0])

    @pl.when(j == pl.num_programs(1) - 1)
    def _():
        if normed_input:
            o_ref[...] = jnp.where(valid, acc_ref[...], 0.0)
        else:
            o_ref[...] = x_ref[...] + acc_ref[...]


def swiglu_tiles(x, g, wg, wu, wd, tile_expert, tile_valid, *, normed_input, tm, tf=512):
    m, d = x.shape
    f = wg.shape[2]
    tf = _row_tile(f, tf)
    assert m % tm == 0
    return pl.pallas_call(
        functools.partial(_ffn_kernel, normed_input=normed_input),
        grid_spec=pltpu.PrefetchScalarGridSpec(
            num_scalar_prefetch=2,
            grid=(m // tm, f // tf),
            in_specs=[pl.BlockSpec((tm, d), lambda i, j, te, tv: (i, 0)),
                      pl.BlockSpec((1, d), lambda i, j, te, tv: (0, 0)),
                      pl.BlockSpec((1, d, tf), lambda i, j, te, tv: (te[i], 0, j)),
                      pl.BlockSpec((1, d, tf), lambda i, j, te, tv: (te[i], 0, j)),
                      pl.BlockSpec((1, tf, d), lambda i, j, te, tv: (te[i], j, 0))],
            out_specs=pl.BlockSpec((tm, d), lambda i, j, te, tv: (i, 0)),
            scratch_shapes=[pltpu.VMEM((tm, d), BF16), pltpu.VMEM((tm, d), F32)]),
        out_shape=jax.ShapeDtypeStruct((m, d), F32),
        compiler_params=_params("parallel", "arbitrary"),
        name="swiglu_tiles",
    )(tile_expert, tile_valid, x, g.reshape(1, d), wg, wu, wd)


def _router_kernel(x_ref, g_ref, wr_ref, h_ref, gates_ref, sel_ref, *, n_e):
    h = _rms(x_ref[...], g_ref[...])
    h_ref[...] = h.astype(BF16)
    logits = jnp.dot(h, wr_ref[...], preferred_element_type=F32, precision=lax.Precision.HIGHEST)
    idx = lax.broadcasted_iota(jnp.int32, logits.shape, 1).astype(F32)
    logits = jnp.where(idx < n_e, logits, -jnp.inf)
    m1 = jnp.max(logits, axis=-1, keepdims=True)
    i1 = jnp.min(jnp.where(logits == m1, idx, float(n_e)), axis=-1, keepdims=True)
    rest = jnp.where(idx == i1, -jnp.inf, logits)
    m2 = jnp.max(rest, axis=-1, keepdims=True)
    i2 = jnp.min(jnp.where(rest == m2, idx, float(n_e)), axis=-1, keepdims=True)
    e2 = jnp.exp(m2 - m1)
    g1 = 1.0 / (1.0 + e2)
    g2 = e2 / (1.0 + e2)
    gates_ref[...] = jnp.where(idx == i1, g1, jnp.where(idx == i2, g2, 0.0))
    sel_ref[...] = jnp.where(idx == i1, 1, jnp.where(idx == i2, 1, 0)).astype(jnp.int32)


def router(x, g, w_router, *, tm=512):
    m, d = x.shape
    n_e = w_router.shape[1]
    assert TOP_K <= n_e <= LANES
    tm = _row_tile(m, tm)
    wr = jnp.pad(w_router, ((0, 0), (0, LANES - n_e)))
    h, gates, sel = pl.pallas_call(
        functools.partial(_router_kernel, n_e=n_e),
        grid=(m // tm,),
        in_specs=[pl.BlockSpec((tm, d), lambda i: (i, 0)),
                  pl.BlockSpec((1, d), lambda i: (0, 0)),
                  pl.BlockSpec((d, LANES), lambda i: (0, 0))],
        out_specs=[pl.BlockSpec((tm, d), lambda i: (i, 0)),
                   pl.BlockSpec((tm, LANES), lambda i: (i, 0)),
                   pl.BlockSpec((tm, LANES), lambda i: (i, 0))],
        out_shape=[jax.ShapeDtypeStruct((m, d), BF16),
                   jax.ShapeDtypeStruct((m, LANES), F32),
                   jax.ShapeDtypeStruct((m, LANES), jnp.int32)],
        compiler_params=_params("parallel"),
        name="router",
    )(x, g.reshape(1, d), wr)
    return h, gates[:, :n_e], sel[:, :n_e]


def _combine_kernel(x_ref, y1_ref, y2_ref, gt_ref, o_ref):
    gt = gt_ref[...]
    o_ref[...] = x_ref[...] + gt[:, 0:1] * y1_ref[...] + gt[:, 1:2] * y2_ref[...]


def moe_combine(x, y1, y2, gates2, *, tm=512):
    m, d = x.shape
    tm = _row_tile(m, tm)
    spec = pl.BlockSpec((tm, d), lambda i: (i, 0))
    return pl.pallas_call(
        _combine_kernel,
        grid=(m // tm,),
        in_specs=[spec, spec, spec, pl.BlockSpec((tm, TOP_K), lambda i: (i, 0))],
        out_specs=spec,
        out_shape=jax.ShapeDtypeStruct((m, d), F32),
        compiler_params=_params("parallel"),
        name="moe_combine",
    )(x, y1, y2, gates2)


MOE_TILE = 512


def moe_layer(xs, g, w_router, wg, wu, wd):
    n_e = w_router.shape[1]
    tm = MOE_TILE
    routed = [router(x, g, w_router) for x in xs]
    h, gates8, sel8 = (jnp.concatenate(parts, axis=0) for parts in zip(*routed))
    m = h.shape[0]
    ids = jnp.argsort(-sel8, axis=-1, stable=True)[:, :TOP_K].astype(jnp.int32)
    gates2 = jnp.take_along_axis(gates8, ids, axis=-1)
    n_slots = m * TOP_K
    n_tiles = (n_slots + n_e * (tm - 1)) // tm
    flat_e = ids.reshape(-1)
    order = jnp.argsort(flat_e, stable=True).astype(jnp.int32)
    counts = jnp.zeros((n_e,), jnp.int32).at[flat_e].add(1)
    padded = ((counts + tm - 1) // tm) * tm
    pad_end = jnp.cumsum(padded)
    pad_start = pad_end - padded
    start = jnp.cumsum(counts) - counts
    e_sorted = flat_e[order]
    pos_sorted = pad_start[e_sorted] + jnp.arange(n_slots, dtype=jnp.int32) - start[e_sorted]
    pos = jnp.zeros((n_slots,), jnp.int32).at[order].set(pos_sorted)
    src_token = jnp.zeros((n_tiles * tm,), jnp.int32).at[pos].set(jnp.arange(n_slots, dtype=jnp.int32) // TOP_K)
    tile_start = jnp.arange(n_tiles, dtype=jnp.int32) * tm
    tile_expert = jnp.minimum(jnp.searchsorted(pad_end, tile_start, side="right"), n_e - 1).astype(jnp.int32)
    tile_valid = (tile_start < pad_end[-1]).astype(jnp.int32)
    h_sorted = jnp.take(h, src_token, axis=0)
    y_sorted = swiglu_tiles(h_sorted, g, wg, wu, wd, tile_expert, tile_valid, normed_input=True, tm=tm)
    pos2 = pos.reshape(m, TOP_K)
    results, row0 = [], 0
    for x in xs:
        rows = slice(row0, row0 + x.shape[0])
        y1 = jnp.take(y_sorted, pos2[rows, 0], axis=0)
        y2 = jnp.take(y_sorted, pos2[rows, 1], axis=0)
        results.append(moe_combine(x, y1, y2, gates2[rows]))
        row0 += x.shape[0]
    return results


def dense_ffn(x, g, wg, wu, wd):
    m = x.shape[0]
    tm = _row_tile(m, 1024)
    n_tiles = m // tm
    return swiglu_tiles(x, g, wg[None], wu[None], wd[None], jnp.zeros((n_tiles,), jnp.int32),
                        jnp.ones((n_tiles,), jnp.int32), normed_input=False, tm=tm)


def _final_norm_kernel(x_ref, g_ref, o_ref):
    o_ref[...] = _rms(x_ref[...], g_ref[...])


def final_norm(x, g, *, tm=512):
    m, d = x.shape
    tm = _row_tile(m, tm)
    return pl.pallas_call(
        _final_norm_kernel,
        grid=(m // tm,),
        in_specs=[pl.BlockSpec((tm, d), lambda i: (i, 0)), pl.BlockSpec((1, d), lambda i: (0, 0))],
        out_specs=pl.BlockSpec((tm, d), lambda i: (i, 0)),
        out_shape=jax.ShapeDtypeStruct((m, d), F32),
        compiler_params=_params("parallel"),
        name="final_norm",
    )(x, g.reshape(1, d))


def _block_diag(w):
    g, c, _ = w.shape
    eye = jnp.eye(g, dtype=w.dtype)
    return (eye[:, None, :, None] * w[:, :, None, :]).reshape(g * c, g * c)


def _pad_time(a, t):
    return jnp.pad(a, ((0, 0), (0, t - a.shape[1]), (0, 0)))


def kernel(x_prompt, x_sample, cache_k, cache_v, cache_mem_k, cache_mem_v, state_pool, state_sconv, state_conf, page_table, mem_prompt, g_mix, w_in, w_pool, pool_scale, w_sconv, sb_bias, w_conf, b_conf, g_conf_ln, b_conf_ln, w_out, g_cross, g_mem, w_mq, w_mk, w_mv, w_mo, g_ffn, w_ff_gate, w_ff_up, w_ff_down, w_router, w_moe_gate, w_moe_up, w_moe_down, g_final):
    bp, tp, d = x_prompt.shape
    bs, ts, _ = x_sample.shape
    depth = g_mix.shape[0]
    c = d // 4
    dh = c // SB_HEADS
    page = cache_k.shape[2]
    past = page_table.shape[1] * page
    n_mem = mem_prompt.shape[1]
    qscale = float(dh ** -0.5)
    assert ts <= SAMPLE_ROWS and ts <= page

    bf = lambda a: a.astype(BF16)
    w_in_b, w_out_b, w_mq_b, w_mk_b, w_mv_b, w_mo_b = map(bf, (w_in, w_out, w_mq, w_mk, w_mv, w_mo))
    cache_kt = cache_k.transpose(0, 1, 3, 4, 2)
    cache_vt = cache_v.transpose(0, 1, 3, 4, 2)

    xp = x_prompt.reshape(bp * tp, d)
    xs = x_sample.reshape(bs * ts, d)
    zeros_state = lambda keep: jnp.zeros((bp, keep, c), F32)
    outs = {name: [] for name in ("kp", "vp", "ks", "vs", "mkp", "mvp", "poolp", "pools", "scp", "scs", "cfp", "cfs")}

    for i in range(depth):
        wpool_bd = bf(_block_diag(w_pool[i]))
        mixer_w = (wpool_bd, pool_scale[i], w_sconv[i], w_conf[i], b_conf[i], g_conf_ln[i], b_conf_ln[i])

        pa, pb, q, k, v, kb, vb, pd = in_proj(xp, g_mix[i], w_in_b[i], qscale=qscale)
        r3 = lambda a: a.reshape(bp, tp, a.shape[-1])
        ya, yb, yd, n_pool, n_sc, n_cf = local_mixers(
            r3(pa), r3(pb), r3(pd), zeros_state(max(POOL_WINDOWS) - 1), zeros_state(w_sconv.shape[1] - 1),
            zeros_state(w_conf.shape[1] - 1), *mixer_w, t_valid=tp, pos0=0)
        yc = sb_prompt(r3(q), r3(kb), r3(vb), sb_bias[i])
        f2 = lambda a: a.reshape(bp * tp, c)
        xp = out_proj(xp, f2(ya), f2(yb), f2(yc), f2(yd), w_out_b[i])
        outs["kp"].append(k.reshape(bp, tp, SB_HEADS, dh))
        outs["vp"].append(v.reshape(bp, tp, SB_HEADS, dh))
        outs["poolp"].append(n_pool); outs["scp"].append(n_sc); outs["cfp"].append(n_cf)

        mem2 = mem_prompt.reshape(bp * n_mem, d)
        mk_p = norm_matmul(mem2, g_mem[i], w_mk_b[i]).reshape(bp, n_mem, MEM_HEADS, d // MEM_HEADS)
        mv_p = norm_matmul(mem2, g_mem[i], w_mv_b[i]).reshape(bp, n_mem, MEM_HEADS, d // MEM_HEADS)
        outs["mkp"].append(mk_p)
        outs["mvp"].append(mv_p)
        xp = cross_block(xp.reshape(bp, tp, d), g_cross[i], w_mq_b[i], mk_p, mv_p, w_mo_b[i]).reshape(bp * tp, d)

        pa, pb, q, k, v, kb, vb, pd = in_proj(xs, g_mix[i], w_in_b[i], qscale=qscale)
        r3 = lambda a: _pad_time(a.reshape(bs, ts, a.shape[-1]), SAMPLE_ROWS)
        ya, yb, yd, n_pool, n_sc, n_cf = local_mixers(
            r3(pa), r3(pb), r3(pd), state_pool[i], state_sconv[i], state_conf[i], *mixer_w,
            t_valid=ts, pos0=past)
        heads_first = lambda a: a.reshape(bs, ts, SB_HEADS, dh).transpose(0, 2, 1, 3)
        q_s = jnp.pad(heads_first(q.astype(F32)), ((0, 0), (0, 0), (0, SAMPLE_ROWS - ts), (0, 0)))
        new_t = lambda a: jnp.pad(heads_first(a).transpose(0, 1, 3, 2), ((0, 0), (0, 0), (0, 0), (0, page - ts)))
        yc = sb_sample(q_s, new_t(k), new_t(v), cache_kt, cache_vt, i, page_table, sb_bias[i])
        yc = yc[:, :, :ts].transpose(0, 2, 1, 3).reshape(bs * ts, c)
        f2 = lambda a: a[:, :ts].reshape(bs * ts, c)
        xs = out_proj(xs, f2(ya), f2(yb), yc, f2(yd), w_out_b[i])
        outs["ks"].append(k.reshape(bs, ts, SB_HEADS, dh))
        outs["vs"].append(v.reshape(bs, ts, SB_HEADS, dh))
        outs["pools"].append(n_pool); outs["scs"].append(n_sc); outs["cfs"].append(n_cf)

        xs_pad = _pad_time(xs.reshape(bs, ts, d), 2 * SUBLANES)
        xs = cross_block(xs_pad, g_cross[i], w_mq_b[i], cache_mem_k[i], cache_mem_v[i],
                         w_mo_b[i])[:, :ts].reshape(bs * ts, d)

        j = i // 2
        if i % 2 == 0:
            wg, wu, wd = bf(w_ff_gate[j]), bf(w_ff_up[j]), bf(w_ff_down[j])
            xp = dense_ffn(xp, g_ffn[i], wg, wu, wd)
            xs = dense_ffn(xs, g_ffn[i], wg, wu, wd)
        else:
            xp, xs = moe_layer([xp, xs], g_ffn[i], w_router[j],
                               bf(w_moe_gate[j]), bf(w_moe_up[j]), bf(w_moe_down[j]))

    y_prompt = final_norm(xp, g_final).reshape(bp, tp, d)
    y_sample = final_norm(xs, g_final).reshape(bs, ts, d)
    st = lambda name: jnp.stack(outs[name])
    return (y_prompt, y_sample, st("kp"), st("vp"), st("ks"), st("vs"), st("mkp"), st("mvp"),
            st("poolp"), st("pools"), st("scp"), st("scs"), st("cfp"), st("cfs"))
```

```python
import functools

import jax
import jax.numpy as jnp
from jax import lax
from jax.experimental import pallas as pl
from jax.experimental.pallas import tpu as pltpu

EPS = 1e-6
SB_HEADS = 4
MEM_HEADS = 4
POOL_WINDOWS = (2, 4, 8, 16)
TOP_K = 2
LANES = 128
SUBLANES = 8
VMEM_LIMIT_BYTES = 56 * 1024 * 1024
BF16 = jnp.bfloat16
F32 = jnp.float32


def _params(*sem):
    return pltpu.CompilerParams(dimension_semantics=sem, vmem_limit_bytes=VMEM_LIMIT_BYTES)


def _rms(x, g):
    return x * lax.rsqrt(jnp.mean(x * x, axis=-1, keepdims=True) + EPS) * g


def _dot(a, b):
    return jnp.dot(a, b, preferred_element_type=F32)


def _dot_nt(a, b):
    return lax.dot_general(a, b, (((1,), (1,)), ((), ())), preferred_element_type=F32)


def _row_tile(m, want):
    if m <= want:
        return m
    for t in range(want - want % SUBLANES, 0, -SUBLANES):
        if m % t == 0:
            return t
    raise ValueError(f"no aligned row tile for {m}")


def _in_proj_kernel(x_ref, g_ref, w_ref, pa_ref, pb_ref, q_ref, k_ref, v_ref, kb_ref, vb_ref, pd_ref,
                    *, c, qscale, kv_transposed):
    h = _rms(x_ref[...], g_ref[...]).astype(BF16)
    pa_ref[...] = _dot(h, w_ref[:, 0:c])
    pb_ref[...] = _dot(h, w_ref[:, c:4 * c])
    q_ref[...] = (_dot(h, w_ref[:, 4 * c:5 * c]) * qscale).astype(BF16)
    k = _dot(h, w_ref[:, 5 * c:6 * c])
    v = _dot(h, w_ref[:, 6 * c:7 * c])
    if kv_transposed:
        k_ref[0] = k.T
        v_ref[0] = v.T
    else:
        k_ref[...] = k
        v_ref[...] = v
    kb_ref[...] = k.astype(BF16)
    vb_ref[...] = v.astype(BF16)
    pd_ref[...] = _dot(h, w_ref[:, 7 * c:9 * c])


def in_proj(x, g, w_bf16, *, qscale, seq_len=None, tm=512):
    m, d = x.shape
    c = w_bf16.shape[1] // 9
    tm = _row_tile(m if seq_len is None else seq_len, tm)
    row = lambda wd: pl.BlockSpec((tm, wd), lambda i: (i, 0))
    rows = lambda wd, dt: jax.ShapeDtypeStruct((m, wd), dt)
    if seq_len is None:
        kv_spec, kv_shape = row(c), rows(c, F32)
    else:
        tiles = seq_len // tm
        kv_spec = pl.BlockSpec((1, c, tm), lambda i: (i // tiles, 0, i % tiles))
        kv_shape = jax.ShapeDtypeStruct((m // seq_len, c, seq_len), F32)
    return pl.pallas_call(
        functools.partial(_in_proj_kernel, c=c, qscale=qscale, kv_transposed=seq_len is not None),
        grid=(m // tm,),
        in_specs=[pl.BlockSpec((tm, d), lambda i: (i, 0)),
                  pl.BlockSpec((1, d), lambda i: (0, 0)),
                  pl.BlockSpec((d, 9 * c), lambda i: (0, 0))],
        out_specs=[row(c), row(3 * c), row(c), kv_spec, kv_spec, row(c), row(c), row(2 * c)],
        out_shape=[rows(c, F32), rows(3 * c, F32), rows(c, BF16), kv_shape, kv_shape, rows(c, BF16), rows(c, BF16),
                   rows(2 * c, F32)],
        compiler_params=_params("parallel"),
        name="in_proj",
    )(x, g.reshape(1, d), w_bf16)


POOL_HALO = 16
SCONV_HALO = 8
CONF_HALO = 32


def _stage(buf_ref, state_ref, new_rows, halo, tt, first):
    @pl.when(first)
    def _():
        buf_ref[0:halo, :] = state_ref[0]
    buf_ref[halo:halo + tt, :] = new_rows


def _carry_history(buf_ref, halo, tt):
    buf_ref[0:halo, :] = buf_ref[tt:tt + halo, :]


def _causal_conv(buf_ref, w_ref, halo, tt, width):
    acc = None
    for j in range(width):
        off = halo - (width - 1) + j
        term = buf_ref[off:off + tt, :] * w_ref[j:j + 1, :]
        acc = term if acc is None else acc + term
    return acc


def _mixers_kernel(pa_ref, pb_ref, pd_ref, spool_ref, ssc_ref, scf_ref,
                   wpool_ref, pscale_ref, wsc_ref, wcf_ref, bcf_ref, gln_ref, bln_ref,
                   ya_ref, yb_ref, yd_ref, npool_ref, nsc_ref, ncf_ref,
                   bufa_ref, bufb_ref, bufd_ref, *, tt, n_tiles, t_valid, pos0, sc_width, cf_width):
    i = pl.program_id(1)
    first = i == 0
    c = ya_ref.shape[-1]

    u = pa_ref[0]
    _stage(bufa_ref, spool_ref, u, POOL_HALO, tt, first)
    sums = {}
    acc = u
    for j in range(1, max(POOL_WINDOWS)):
        acc = acc + bufa_ref[POOL_HALO - j:POOL_HALO - j + tt, :]
        if j + 1 in POOL_WINDOWS:
            sums[j + 1] = acc
    lane = lax.broadcasted_iota(jnp.int32, (tt, c), 1)
    grp = lane // (c // len(POOL_WINDOWS))
    win = sums[POOL_WINDOWS[-1]]
    wlen = jnp.full((tt, c), POOL_WINDOWS[-1], jnp.int32)
    for gi in range(len(POOL_WINDOWS) - 2, -1, -1):
        win = jnp.where(grp == gi, sums[POOL_WINDOWS[gi]], win)
        wlen = jnp.where(grp == gi, POOL_WINDOWS[gi], wlen)
    pos = pos0 + i * tt + lax.broadcasted_iota(jnp.int32, (tt, c), 0)
    cnt = jnp.minimum(pos + 1, wlen).astype(F32)
    p = win / cnt - u
    ya_ref[0] = _dot(p.astype(BF16), wpool_ref[...]) * pscale_ref[...]

    pb = pb_ref[0]
    hb, cb, bb = pb[:, 0:c], pb[:, c:2 * c], pb[:, 2 * c:3 * c]
    _stage(bufb_ref, ssc_ref, cb * hb, SCONV_HALO, tt, first)
    yb_ref[0] = bb * _causal_conv(bufb_ref, wsc_ref, SCONV_HALO, tt, sc_width)

    pd = pd_ref[0]
    ad, gd = pd[:, 0:c], pd[:, c:2 * c]
    _stage(bufd_ref, scf_ref, ad * jax.nn.sigmoid(gd), CONF_HALO, tt, first)
    cv = _causal_conv(bufd_ref, wcf_ref, CONF_HALO, tt, cf_width) + bcf_ref[...]
    xc = cv - jnp.mean(cv, axis=-1, keepdims=True)
    ln = xc * lax.rsqrt(jnp.mean(xc * xc, axis=-1, keepdims=True) + EPS) * gln_ref[...] + bln_ref[...]
    yd_ref[0] = ln * jax.nn.sigmoid(ln)

    @pl.when(i == n_tiles - 1)
    def _():
        for out_ref, buf_ref, halo, keep in ((npool_ref, bufa_ref, POOL_HALO, max(POOL_WINDOWS) - 1),
                                             (nsc_ref, bufb_ref, SCONV_HALO, sc_width - 1),
                                             (ncf_ref, bufd_ref, CONF_HALO, cf_width - 1)):
            end = halo + t_valid
            out_ref[0] = buf_ref[end - keep:end, :]

    if n_tiles > 1:
        _carry_history(bufa_ref, POOL_HALO, tt)
        _carry_history(bufb_ref, SCONV_HALO, tt)
        _carry_history(bufd_ref, CONF_HALO, tt)


def _pad_state(state, halo):
    return jnp.pad(state, ((0, 0), (halo - state.shape[1], 0), (0, 0)))


def local_mixers(pa, pb, pd, st_pool, st_sc, st_cf, wpool_bd, pool_scale, w_sc, w_cf, b_cf, g_ln, b_ln,
                 *, t_valid, pos0, tt=512):
    b, t, c = pa.shape
    tt = _row_tile(t, tt)
    n_tiles = t // tt
    assert n_tiles == 1 or tt >= CONF_HALO
    sc_width, cf_width = w_sc.shape[0], w_cf.shape[0]
    keep_pool = max(POOL_WINDOWS) - 1
    tile = lambda wd: pl.BlockSpec((1, tt, wd), lambda bi, i: (bi, i, 0))
    per_b = lambda r, wd: pl.BlockSpec((1, r, wd), lambda bi, i: (bi, 0, 0))
    full2 = lambda a: pl.BlockSpec(a.shape, lambda bi, i: (0, 0))
    row = lambda a: a.reshape(1, c)
    weights = (wpool_bd, row(pool_scale), w_sc, w_cf, row(b_cf), row(g_ln), row(b_ln))
    last_valid = t_valid - (n_tiles - 1) * tt
    return pl.pallas_call(
        functools.partial(_mixers_kernel, tt=tt, n_tiles=n_tiles, t_valid=last_valid, pos0=pos0,
                          sc_width=sc_width, cf_width=cf_width),
        grid=(b, n_tiles),
        in_specs=[tile(c), tile(3 * c), tile(2 * c),
                  per_b(POOL_HALO, c), per_b(SCONV_HALO, c), per_b(CONF_HALO, c)] + [full2(a) for a in weights],
        out_specs=[tile(c), tile(c), tile(c),
                   per_b(keep_pool, c), per_b(sc_width - 1, c), per_b(cf_width - 1, c)],
        out_shape=[jax.ShapeDtypeStruct((b, t, c), F32)] * 3 + [
            jax.ShapeDtypeStruct((b, keep_pool, c), F32),
            jax.ShapeDtypeStruct((b, sc_width - 1, c), F32),
            jax.ShapeDtypeStruct((b, cf_width - 1, c), F32)],
        scratch_shapes=[pltpu.VMEM((POOL_HALO + tt, c), F32),
                        pltpu.VMEM((SCONV_HALO + tt, c), F32),
                        pltpu.VMEM((CONF_HALO + tt, c), F32)],
        compiler_params=_params("parallel", "arbitrary"),
        name="local_mixers",
    )(pa, pb, pd, _pad_state(st_pool, POOL_HALO), _pad_state(st_sc, SCONV_HALO), _pad_state(st_cf, CONF_HALO),
      *weights)


def _suffix_matrix(tk):
    r = lax.broadcasted_iota(jnp.int32, (2 * tk, tk), 0) % tk
    s = lax.broadcasted_iota(jnp.int32, (2 * tk, tk), 1)
    return jnp.where(r > s, 1.0, 0.0).astype(BF16)


LOG2E = 1.4426950408889634
_SIGN_BIT = 0x80000000


def _softplus2(z2, mask):
    neg_abs = lax.bitcast_convert_type(lax.bitcast_convert_type(z2, jnp.uint32) | jnp.uint32(_SIGN_BIT), F32)
    sp = jnp.maximum(z2, 0.0) + jnp.log2(1.0 + jnp.exp2(neg_abs))
    return sp if mask is None else jnp.where(mask, sp, 0.0)


def _later_sum(sp, suffix):
    hi = sp.astype(BF16)
    lo = (sp - hi.astype(F32)).astype(BF16)
    return _dot(jnp.concatenate([hi, lo], axis=1), suffix)


def _sb_weight(z2, sp, later, carry, mask):
    a = jnp.exp2(z2 - sp - later - carry)
    return (a if mask is None else jnp.where(mask, a, 0.0)).astype(BF16)


def _head_select(x, c, heads):
    lane_head = lax.broadcasted_iota(jnp.int32, x.shape, x.ndim - 1) // (c // heads)
    return [jnp.where(lane_head == h, x, jnp.zeros_like(x)) for h in range(heads)]


def _sb_prompt_kernel(bias_ref, q_ref, k_ref, v_ref, o_ref, acc_ref, *, tq):
    qi = pl.program_id(1)
    c = q_ref.shape[-1]
    qh = _head_select(q_ref[0], c, SB_HEADS)
    suffix = _suffix_matrix(tq)
    row = lax.broadcasted_iota(jnp.int32, (tq, tq), 0)
    col = lax.broadcasted_iota(jnp.int32, (tq, tq), 1)
    diag_mask = col < row

    bias2 = [bias_ref[h] * LOG2E for h in range(SB_HEADS)]
    heads = range(SB_HEADS)

    def key_blocks(kbis, carries, mask, first):
        starts = [pl.multiple_of(kbi * tq, tq) for kbi in kbis]
        z2 = [[_dot_nt(qh[h], k_ref[0, pl.ds(s, tq), :]) * LOG2E + bias2[h] for h in heads] for s in starts]
        sp = [[_softplus2(z, mask) for z in zb] for zb in z2]
        later = [[_later_sum(x, suffix) for x in sb] for sb in sp]
        carries = list(carries)
        outs = [None] * SB_HEADS
        for bi, s in enumerate(starts):
            vb = v_ref[0, pl.ds(s, tq), :]
            for h in heads:
                o = _dot(_sb_weight(z2[bi][h], sp[bi][h], later[bi][h], carries[h], mask), vb)
                outs[h] = o if outs[h] is None else outs[h] + o
                carries[h] = carries[h] + jnp.sum(sp[bi][h], axis=1, keepdims=True)
        for h in heads:
            if first:
                acc_ref[h] = outs[h]
            else:
                acc_ref[h] += outs[h]
        return tuple(carries)

    zero = jnp.zeros((tq, 1), F32)
    carries = key_blocks([qi], (zero,) * SB_HEADS, diag_mask, True)
    n_pairs = lax.shift_right_logical(qi, 1)
    carries = lax.fori_loop(
        0, n_pairs, lambda j, cs: key_blocks([qi - 1 - 2 * j, qi - 2 - 2 * j], cs, None, False), carries)

    @pl.when(lax.bitwise_and(qi, 1) == 1)
    def _():
        key_blocks([0], carries, None, False)

    parts = [_head_select(acc_ref[h], c, SB_HEADS)[h] for h in range(SB_HEADS)]
    o_ref[0] = functools.reduce(lambda x, y: x + y, parts)


def sb_prompt(q, k, v, bias, *, tq=256):
    b, t, c = q.shape
    tq = _row_tile(t, tq)
    return pl.pallas_call(
        functools.partial(_sb_prompt_kernel, tq=tq),
        grid_spec=pltpu.PrefetchScalarGridSpec(
            num_scalar_prefetch=1,
            grid=(b, t // tq),
            in_specs=[pl.BlockSpec((1, tq, c), lambda bi, i, *_: (bi, i, 0)),
                      pl.BlockSpec((1, t, c), lambda bi, i, *_: (bi, 0, 0)),
                      pl.BlockSpec((1, t, c), lambda bi, i, *_: (bi, 0, 0))],
            out_specs=pl.BlockSpec((1, tq, c), lambda bi, i, *_: (bi, i, 0)),
            scratch_shapes=[pltpu.VMEM((SB_HEADS, tq, c), F32)]),
        out_shape=jax.ShapeDtypeStruct((b, t, c), F32),
        compiler_params=_params("parallel", "arbitrary"),
        name="sb_prompt",
    )(bias, q, k, v)


SAMPLE_ROWS = 16
PAGES_PER_STEP = 16


def _sb_sample_kernel(pt_ref, bias_ref, q_ref, kn_ref, vn_ref, *rest, n_pages_step):
    k_refs = rest[:n_pages_step]
    v_refs = rest[n_pages_step:2 * n_pages_step]
    o_ref, acc_ref, carry_ref = rest[2 * n_pages_step:]
    j = pl.program_id(1)
    page = kn_ref.shape[-1]
    rows = SB_HEADS * SAMPLE_ROWS
    q_bd = q_ref[0].astype(BF16)
    row = lax.broadcasted_iota(jnp.int32, (rows, 1), 0)
    bias2 = jnp.zeros((rows, 1), F32)
    for h in range(SB_HEADS):
        bias2 = jnp.where(row // SAMPLE_ROWS == h, bias_ref[h] * LOG2E, bias2)
    suffix = _suffix_matrix(page)

    def pages(kt_refs, vt_refs, mask):
        z2 = [_dot(q_bd, kt[...].astype(BF16)) * LOG2E + bias2 for kt in kt_refs]
        sp = [_softplus2(z, mask) for z in z2]
        later = [_later_sum(x, suffix) for x in sp]
        carry = carry_ref[...]
        out = None
        for z, x, lt, vt in zip(z2, sp, later, vt_refs):
            o = _dot_nt(_sb_weight(z, x, lt, carry, mask), vt[...].astype(BF16))
            out = o if out is None else out + o
            carry = carry + jnp.sum(x, axis=1, keepdims=True)
        carry_ref[...] = carry
        return out

    @pl.when(j == 0)
    def _():
        carry_ref[...] = jnp.zeros_like(carry_ref)
        t_new = lax.broadcasted_iota(jnp.int32, (rows, page), 0) % SAMPLE_ROWS
        s_new = lax.broadcasted_iota(jnp.int32, (rows, page), 1)
        acc_ref[...] = pages([kn_ref.at[0]], [vn_ref.at[0]], s_new < t_new)

    acc_ref[...] += pages(k_refs, v_refs, None)

    @pl.when(j == pl.num_programs(1) - 1)
    def _():
        c = acc_ref.shape[-1]
        parts = [_head_select(acc_ref[h * SAMPLE_ROWS:(h + 1) * SAMPLE_ROWS, :], c, SB_HEADS)[h]
                 for h in range(SB_HEADS)]
        o_ref[0] = functools.reduce(lambda x, y: x + y, parts)


def sb_sample(q_bd, kt_new, vt_new, cache_kt, cache_vt, layer, page_table, bias):
    b, rows, c = q_bd.shape
    assert rows == SB_HEADS * SAMPLE_ROWS
    page = cache_kt.shape[-1]
    n_pages = page_table.shape[1]
    pps = PAGES_PER_STEP if n_pages % PAGES_PER_STEP == 0 else 1
    n_steps = n_pages // pps

    def page_spec(p):
        return pl.BlockSpec((None, None, c, page),
                            lambda bi, j, pt, bs: (layer, pt[bi, n_pages - 1 - (j * pps + p)], 0, 0))

    per_b = lambda r, w: pl.BlockSpec((1, r, w), lambda bi, j, *_: (bi, 0, 0))
    return pl.pallas_call(
        functools.partial(_sb_sample_kernel, n_pages_step=pps),
        grid_spec=pltpu.PrefetchScalarGridSpec(
            num_scalar_prefetch=2,
            grid=(b, n_steps),
            in_specs=[per_b(rows, c), per_b(c, page), per_b(c, page)]
                     + [page_spec(p) for p in range(pps)] + [page_spec(p) for p in range(pps)],
            out_specs=per_b(SAMPLE_ROWS, c),
            scratch_shapes=[pltpu.VMEM((rows, c), F32), pltpu.VMEM((rows, 1), F32)]),
        out_shape=jax.ShapeDtypeStruct((b, SAMPLE_ROWS, c), F32),
        compiler_params=_params("parallel", "arbitrary"),
        name="sb_sample",
    )(page_table, bias, q_bd, kt_new, vt_new, *([cache_kt] * pps), *([cache_vt] * pps))


def _out_proj_kernel(x_ref, ya_ref, yb_ref, yc_ref, yd_ref, w_ref, o_ref):
    y = jnp.concatenate([r[...].astype(BF16) for r in (ya_ref, yb_ref, yc_ref, yd_ref)], axis=1)
    o_ref[...] = x_ref[...] + _dot(y, w_ref[...])


def out_proj(x, ya, yb, yc, yd, w_bf16, *, tm=512):
    m, d = x.shape
    c = ya.shape[1]
    tm = _row_tile(m, tm)
    return pl.pallas_call(
        _out_proj_kernel,
        grid=(m // tm,),
        in_specs=[pl.BlockSpec((tm, d), lambda i: (i, 0))] + [pl.BlockSpec((tm, c), lambda i: (i, 0))] * 4
                 + [pl.BlockSpec((d, d), lambda i: (0, 0))],
        out_specs=pl.BlockSpec((tm, d), lambda i: (i, 0)),
        out_shape=jax.ShapeDtypeStruct((m, d), F32),
        compiler_params=_params("parallel"),
        name="out_proj",
    )(x, ya, yb, yc, yd, w_bf16)


def _norm_matmul_kernel(x_ref, g_ref, w_ref, o_ref):
    o_ref[...] = _dot(_rms(x_ref[...], g_ref[...]).astype(BF16), w_ref[...])


def norm_matmul(x, g, w_bf16, *, tm=512):
    m, d = x.shape
    n = w_bf16.shape[1]
    tm = _row_tile(m, tm)
    return pl.pallas_call(
        _norm_matmul_kernel,
        grid=(m // tm,),
        in_specs=[pl.BlockSpec((tm, d), lambda i: (i, 0)),
                  pl.BlockSpec((1, d), lambda i: (0, 0)),
                  pl.BlockSpec((d, n), lambda i: (0, 0))],
        out_specs=pl.BlockSpec((tm, n), lambda i: (i, 0)),
        out_shape=jax.ShapeDtypeStruct((m, n), F32),
        compiler_params=_params("parallel"),
        name="norm_matmul",
    )(x, g.reshape(1, d), w_bf16)


def _cross_kernel(x_ref, g_ref, wq_ref, mk_ref, mv_ref, wo_ref, o_ref, *, scale):
    x = x_ref[0]
    d = x.shape[-1]
    dh = d // MEM_HEADS
    q = (_dot(_rms(x, g_ref[...]).astype(BF16), wq_ref[...]) * scale).astype(BF16)
    outs = []
    for h in range(MEM_HEADS):
        s = _dot_nt(q[:, h * dh:(h + 1) * dh], mk_ref[0, :, h, :].astype(BF16))
        e = jnp.exp(s - jnp.max(s, axis=-1, keepdims=True))
        pr = e / jnp.sum(e, axis=-1, keepdims=True)
        outs.append(_dot(pr.astype(BF16), mv_ref[0, :, h, :].astype(BF16)).astype(BF16))
    o_ref[0] = x + _dot(jnp.concatenate(outs, axis=1), wo_ref[...])


def cross_block(x, g, wq_bf16, mk, mv, wo_bf16, *, tm=512):
    b, t, d = x.shape
    _, n_mem, heads, dh = mk.shape
    assert heads == MEM_HEADS and heads * dh == d
    tm = _row_tile(t, tm)
    scale = float(dh ** -0.5)
    const = lambda a: pl.BlockSpec(a.shape, lambda bi, i: (0,) * a.ndim)
    g2 = g.reshape(1, d)
    mem_spec = pl.BlockSpec((1, n_mem, heads, dh), lambda bi, i: (bi, 0, 0, 0))
    return pl.pallas_call(
        functools.partial(_cross_kernel, scale=scale),
        grid=(b, t // tm),
        in_specs=[pl.BlockSpec((1, tm, d), lambda bi, i: (bi, i, 0)), const(g2), const(wq_bf16),
                  mem_spec, mem_spec, const(wo_bf16)],
        out_specs=pl.BlockSpec((1, tm, d), lambda bi, i: (bi, i, 0)),
        out_shape=jax.ShapeDtypeStruct((b, t, d), F32),
        compiler_params=_params("parallel", "parallel"),
        name="cross_block",
    )(x, g2, wq_bf16, mk, mv, wo_bf16)


def _swiglu_chunk(h_ref, wg_ref, wu_ref, wd_ref, acc_ref):
    h = h_ref[...]
    gate = _dot(h, wg_ref[0])
    up = _dot(h, wu_ref[0])
    act = gate * jax.nn.sigmoid(gate) * up
    acc_ref[...] += _dot(act.astype(BF16), wd_ref[0])


def _ffn_kernel(x_ref, g_ref, wg_ref, wu_ref, wd_ref, o_ref, h_ref, acc_ref):
    j = pl.program_id(1)

    @pl.when(j == 0)
    def _():
        h_ref[...] = _rms(x_ref[...], g_ref[...]).astype(BF16)
        acc_ref[...] = jnp.zeros_like(acc_ref)

    _swiglu_chunk(h_ref, wg_ref, wu_ref, wd_ref, acc_ref)

    @pl.when(j == pl.num_programs(1) - 1)
    def _():
        o_ref[...] = x_ref[...] + acc_ref[...]


def dense_ffn(x, g, wg, wu, wd, *, tm=1024, tf=512):
    m, d = x.shape
    f = wg.shape[1]
    tm, tf = _row_tile(m, tm), _row_tile(f, tf)
    return pl.pallas_call(
        _ffn_kernel,
        grid=(m // tm, f // tf),
        in_specs=[pl.BlockSpec((tm, d), lambda i, j: (i, 0)),
                  pl.BlockSpec((1, d), lambda i, j: (0, 0)),
                  pl.BlockSpec((1, d, tf), lambda i, j: (0, 0, j)),
                  pl.BlockSpec((1, d, tf), lambda i, j: (0, 0, j)),
                  pl.BlockSpec((1, tf, d), lambda i, j: (0, j, 0))],
        out_specs=pl.BlockSpec((tm, d), lambda i, j: (i, 0)),
        scratch_shapes=[pltpu.VMEM((tm, d), BF16), pltpu.VMEM((tm, d), F32)],
        out_shape=jax.ShapeDtypeStruct((m, d), F32),
        compiler_params=_params("parallel", "arbitrary"),
        name="dense_ffn",
    )(x, g.reshape(1, d), wg[None], wu[None], wd[None])


def _moe_ffn_kernel(te_ref, src_ref, x_hbm, wg_ref, wu_ref, wd_ref, o_ref, xbuf_ref, sem_ref, h_ref, acc_ref,
                    *, tm, rows_per_step):
    i, j = pl.program_id(0), pl.program_id(1)
    slot = lax.rem(i, 2)
    last_step = jnp.logical_and(i == pl.num_programs(0) - 1, j == pl.num_programs(1) - 1)

    def start_rows(tile, first_row):
        tile_slot = lax.rem(tile, 2)
        for r in range(rows_per_step):
            row = first_row + r
            token = src_ref[tile * tm + row]
            pltpu.make_async_copy(x_hbm.at[pl.ds(token, 1)], xbuf_ref.at[tile_slot, pl.ds(row, 1)],
                                  sem_ref.at[tile_slot]).start()

    def wait_half(half):
        pltpu.make_async_copy(x_hbm.at[pl.ds(0, tm)], xbuf_ref.at[half], sem_ref.at[half]).wait()

    @pl.when(jnp.logical_and(i == 0, j == 0))
    def _():
        def body(s, carry):
            start_rows(0, s * rows_per_step)
            return carry
        lax.fori_loop(0, tm // rows_per_step, body, 0)

    @pl.when(j == 0)
    def _():
        wait_half(slot)
        h_ref[...] = xbuf_ref[slot].astype(BF16)
        acc_ref[...] = jnp.zeros_like(acc_ref)

    start_rows(i + 1, j * rows_per_step)
    _swiglu_chunk(h_ref, wg_ref, wu_ref, wd_ref, acc_ref)

    @pl.when(j == pl.num_programs(1) - 1)
    def _():
        o_ref[...] = acc_ref[...]

    @pl.when(last_step)
    def _():
        wait_half(1 - slot)


def moe_ffn(x, src_rows, wg, wu, wd, tile_expert, *, tm, chunks=4):
    n_rows = src_rows.shape[0]
    d, f = wg.shape[1:]
    assert n_rows % tm == 0 and f % chunks == 0 and tm % chunks == 0 and x.shape[0] >= tm
    tf = f // chunks
    src_padded = jnp.concatenate([src_rows, jnp.zeros((tm,), src_rows.dtype)])
    return pl.pallas_call(
        functools.partial(_moe_ffn_kernel, tm=tm, rows_per_step=tm // chunks),
        grid_spec=pltpu.PrefetchScalarGridSpec(
            num_scalar_prefetch=2,
            grid=(n_rows // tm, chunks),
            in_specs=[pl.BlockSpec(memory_space=pl.ANY),
                      pl.BlockSpec((1, d, tf), lambda i, j, te, src: (te[i], 0, j)),
                      pl.BlockSpec((1, d, tf), lambda i, j, te, src: (te[i], 0, j)),
                      pl.BlockSpec((1, tf, d), lambda i, j, te, src: (te[i], j, 0))],
            out_specs=pl.BlockSpec((tm, d), lambda i, j, te, src: (i, 0)),
            scratch_shapes=[pltpu.VMEM((2, tm, d), F32), pltpu.SemaphoreType.DMA((2,)),
                            pltpu.VMEM((tm, d), BF16), pltpu.VMEM((tm, d), F32)]),
        out_shape=jax.ShapeDtypeStruct((n_rows, d), F32),
        compiler_params=_params("arbitrary", "arbitrary"),
        name="moe_ffn",
    )(tile_expert, src_padded, x, wg, wu, wd)


def _router_kernel(x_ref, g_ref, wr_ref, h_ref, gates_ref, sel_ref, *, n_e):
    h = _rms(x_ref[...], g_ref[...])
    h_ref[...] = h
    logits = jnp.dot(h, wr_ref[...], preferred_element_type=F32, precision=lax.Precision.HIGHEST)
    idx = lax.broadcasted_iota(jnp.int32, logits.shape, 1).astype(F32)
    logits = jnp.where(idx < n_e, logits, -jnp.inf)
    m1 = jnp.max(logits, axis=-1, keepdims=True)
    i1 = jnp.min(jnp.where(logits == m1, idx, float(n_e)), axis=-1, keepdims=True)
    rest = jnp.where(idx == i1, -jnp.inf, logits)
    m2 = jnp.max(rest, axis=-1, keepdims=True)
    i2 = jnp.min(jnp.where(rest == m2, idx, float(n_e)), axis=-1, keepdims=True)
    e2 = jnp.exp(m2 - m1)
    g1 = 1.0 / (1.0 + e2)
    g2 = e2 / (1.0 + e2)
    gates_ref[...] = jnp.where(idx == i1, g1, jnp.where(idx == i2, g2, 0.0))
    sel_ref[...] = jnp.where(idx == i1, 1, jnp.where(idx == i2, 1, 0)).astype(jnp.int32)


def router(x, g, w_router, *, tm=512):
    m, d = x.shape
    n_e = w_router.shape[1]
    assert TOP_K <= n_e <= LANES
    tm = _row_tile(m, tm)
    wr = jnp.pad(w_router, ((0, 0), (0, LANES - n_e)))
    h, gates, sel = pl.pallas_call(
        functools.partial(_router_kernel, n_e=n_e),
        grid=(m // tm,),
        in_specs=[pl.BlockSpec((tm, d), lambda i: (i, 0)),
                  pl.BlockSpec((1, d), lambda i: (0, 0)),
                  pl.BlockSpec((d, LANES), lambda i: (0, 0))],
        out_specs=[pl.BlockSpec((tm, d), lambda i: (i, 0)),
                   pl.BlockSpec((tm, LANES), lambda i: (i, 0)),
                   pl.BlockSpec((tm, LANES), lambda i: (i, 0))],
        out_shape=[jax.ShapeDtypeStruct((m, d), F32),
                   jax.ShapeDtypeStruct((m, LANES), F32),
                   jax.ShapeDtypeStruct((m, LANES), jnp.int32)],
        compiler_params=_params("parallel"),
        name="router",
    )(x, g.reshape(1, d), wr)
    return h, gates[:, :n_e], sel[:, :n_e]


def _combine_kernel(x_ref, y1_ref, y2_ref, gt_ref, g_ref, o_ref, *, normed):
    gt = gt_ref[...]
    y = x_ref[...] + gt[:, 0:1] * y1_ref[...] + gt[:, 1:2] * y2_ref[...]
    o_ref[...] = _rms(y, g_ref[...]) if normed else y


def moe_combine(x, y1, y2, gates2, g_out, *, tm=512):
    m, d = x.shape
    tm = _row_tile(m, tm)
    spec = pl.BlockSpec((tm, d), lambda i: (i, 0))
    normed = g_out is not None
    g_row = (g_out if normed else jnp.ones((d,), F32)).reshape(1, d)
    return pl.pallas_call(
        functools.partial(_combine_kernel, normed=normed),
        grid=(m // tm,),
        in_specs=[spec, spec, spec, pl.BlockSpec((tm, TOP_K), lambda i: (i, 0)),
                  pl.BlockSpec((1, d), lambda i: (0, 0))],
        out_specs=spec,
        out_shape=jax.ShapeDtypeStruct((m, d), F32),
        compiler_params=_params("parallel"),
        name="moe_combine",
    )(x, y1, y2, gates2, g_row)


MOE_TILE = 512


def moe_layer(xs, g, w_router, wg, wu, wd, g_out=None):
    n_e = w_router.shape[1]
    tm = MOE_TILE
    routed = [router(x, g, w_router) for x in xs]
    h, gates8, sel8 = (jnp.concatenate(parts, axis=0) for parts in zip(*routed))
    m = h.shape[0]
    lanes = jnp.arange(n_e, dtype=jnp.int32)
    ids = jnp.stack([jnp.min(jnp.where(sel8 > 0, lanes, n_e), axis=-1),
                     jnp.max(jnp.where(sel8 > 0, lanes, -1), axis=-1)], axis=-1)
    gates2 = jnp.take_along_axis(gates8, ids, axis=-1)
    n_tiles = (m * TOP_K + n_e * (tm - 1)) // tm
    rank = jnp.cumsum(sel8, axis=0) - sel8
    counts = rank[-1] + sel8[-1]
    padded = ((counts + tm - 1) // tm) * tm
    pad_end = jnp.cumsum(padded)
    pad_start = pad_end - padded
    pos2 = jnp.take_along_axis(pad_start[None, :] + rank, ids, axis=-1)
    token = jnp.broadcast_to(jnp.arange(m, dtype=jnp.int32)[:, None], (m, TOP_K))
    src_token = jnp.zeros((n_tiles * tm,), jnp.int32).at[pos2.reshape(-1)].set(token.reshape(-1))
    tile_start = jnp.arange(n_tiles, dtype=jnp.int32) * tm
    tile_expert = jnp.minimum(jnp.sum(tile_start[:, None] >= pad_end[None, :], axis=-1), n_e - 1).astype(jnp.int32)
    y_sorted = moe_ffn(h, src_token, wg, wu, wd, tile_expert, tm=tm)
    results, row0 = [], 0
    for x in xs:
        rows = slice(row0, row0 + x.shape[0])
        y1 = jnp.take(y_sorted, pos2[rows, 0], axis=0)
        y2 = jnp.take(y_sorted, pos2[rows, 1], axis=0)
        results.append(moe_combine(x, y1, y2, gates2[rows], g_out))
        row0 += x.shape[0]
    return results


def _final_norm_kernel(x_ref, g_ref, o_ref):
    o_ref[...] = _rms(x_ref[...], g_ref[...])


def final_norm(x, g, *, tm=512):
    m, d = x.shape
    tm = _row_tile(m, tm)
    return pl.pallas_call(
        _final_norm_kernel,
        grid=(m // tm,),
        in_specs=[pl.BlockSpec((tm, d), lambda i: (i, 0)), pl.BlockSpec((1, d), lambda i: (0, 0))],
        out_specs=pl.BlockSpec((tm, d), lambda i: (i, 0)),
        out_shape=jax.ShapeDtypeStruct((m, d), F32),
        compiler_params=_params("parallel"),
        name="final_norm",
    )(x, g.reshape(1, d))


def _block_diag(w):
    g, c, _ = w.shape
    eye = jnp.eye(g, dtype=w.dtype)
    return (eye[:, None, :, None] * w[:, :, None, :]).reshape(g * c, g * c)


def _pad_time(a, t):
    return jnp.pad(a, ((0, 0), (0, t - a.shape[1]), (0, 0)))


def kernel(x_prompt, x_sample, cache_k, cache_v, cache_mem_k, cache_mem_v, state_pool, state_sconv, state_conf, page_table, mem_prompt, g_mix, w_in, w_pool, pool_scale, w_sconv, sb_bias, w_conf, b_conf, g_conf_ln, b_conf_ln, w_out, g_cross, g_mem, w_mq, w_mk, w_mv, w_mo, g_ffn, w_ff_gate, w_ff_up, w_ff_down, w_router, w_moe_gate, w_moe_up, w_moe_down, g_final):
    bp, tp, d = x_prompt.shape
    bs, ts, _ = x_sample.shape
    depth = g_mix.shape[0]
    c = d // 4
    dh = c // SB_HEADS
    page = cache_k.shape[2]
    past = page_table.shape[1] * page
    n_mem = mem_prompt.shape[1]
    qscale = float(dh ** -0.5)
    assert ts <= SAMPLE_ROWS and ts <= page

    bf = lambda a: a.astype(BF16)
    w_in_b, w_out_b, w_mq_b, w_mk_b, w_mv_b, w_mo_b = map(bf, (w_in, w_out, w_mq, w_mk, w_mv, w_mo))
    cache_kt = cache_k.transpose(0, 1, 3, 4, 2).reshape(depth, -1, c, page)
    cache_vt = cache_v.transpose(0, 1, 3, 4, 2).reshape(depth, -1, c, page)
    head_mask = (jnp.arange(c)[None, :] // dh == jnp.arange(SB_HEADS)[:, None]).astype(F32)

    xp = x_prompt.reshape(bp * tp, d)
    xs = x_sample.reshape(bs * ts, d)
    zeros_state = lambda keep: jnp.zeros((bp, keep, c), F32)
    outs = {name: [] for name in ("kp", "vp", "ks", "vs", "mkp", "mvp", "poolp", "pools", "scp", "scs", "cfp", "cfs")}

    for i in range(depth):
        wpool_bd = bf(_block_diag(w_pool[i]))
        mixer_w = (wpool_bd, pool_scale[i], w_sconv[i], w_conf[i], b_conf[i], g_conf_ln[i], b_conf_ln[i])

        pa, pb, q, k_t, v_t, kb, vb, pd = in_proj(xp, g_mix[i], w_in_b[i], qscale=qscale, seq_len=tp)
        r3 = lambda a: a.reshape(bp, tp, a.shape[-1])
        ya, yb, yd, n_pool, n_sc, n_cf = local_mixers(
            r3(pa), r3(pb), r3(pd), zeros_state(max(POOL_WINDOWS) - 1), zeros_state(w_sconv.shape[1] - 1),
            zeros_state(w_conf.shape[1] - 1), *mixer_w, t_valid=tp, pos0=0)
        yc = sb_prompt(r3(q), r3(kb), r3(vb), sb_bias[i])
        f2 = lambda a: a.reshape(bp * tp, c)
        xp = out_proj(xp, f2(ya), f2(yb), f2(yc), f2(yd), w_out_b[i])
        heads_last = lambda a_t: a_t.reshape(bp, SB_HEADS, dh, tp).transpose(0, 3, 1, 2)
        outs["kp"].append(heads_last(k_t))
        outs["vp"].append(heads_last(v_t))
        outs["poolp"].append(n_pool); outs["scp"].append(n_sc); outs["cfp"].append(n_cf)

        mem2 = mem_prompt.reshape(bp * n_mem, d)
        mk_p = norm_matmul(mem2, g_mem[i], w_mk_b[i]).reshape(bp, n_mem, MEM_HEADS, d // MEM_HEADS)
        mv_p = norm_matmul(mem2, g_mem[i], w_mv_b[i]).reshape(bp, n_mem, MEM_HEADS, d // MEM_HEADS)
        outs["mkp"].append(mk_p)
        outs["mvp"].append(mv_p)
        xp = cross_block(xp.reshape(bp, tp, d), g_cross[i], w_mq_b[i], mk_p, mv_p, w_mo_b[i]).reshape(bp * tp, d)

        pa, pb, q, k, v, kb, vb, pd = in_proj(xs, g_mix[i], w_in_b[i], qscale=qscale)
        r3 = lambda a: _pad_time(a.reshape(bs, ts, a.shape[-1]), SAMPLE_ROWS)
        ya, yb, yd, n_pool, n_sc, n_cf = local_mixers(
            r3(pa), r3(pb), r3(pd), state_pool[i], state_sconv[i], state_conf[i], *mixer_w,
            t_valid=ts, pos0=past)
        q_s = _pad_time(q.astype(F32).reshape(bs, ts, c), SAMPLE_ROWS)
        q_bd = (q_s[:, None] * head_mask[None, :, None]).reshape(bs, SB_HEADS * SAMPLE_ROWS, c)
        new_t = lambda a: jnp.pad(a.reshape(bs, ts, c).transpose(0, 2, 1), ((0, 0), (0, 0), (0, page - ts)))
        yc = sb_sample(q_bd, new_t(k), new_t(v), cache_kt, cache_vt, i, page_table, sb_bias[i])
        yc = yc[:, :ts].reshape(bs * ts, c)
        f2 = lambda a: a[:, :ts].reshape(bs * ts, c)
        xs = out_proj(xs, f2(ya), f2(yb), yc, f2(yd), w_out_b[i])
        outs["ks"].append(k.reshape(bs, ts, SB_HEADS, dh))
        outs["vs"].append(v.reshape(bs, ts, SB_HEADS, dh))
        outs["pools"].append(n_pool); outs["scs"].append(n_sc); outs["cfs"].append(n_cf)

        xs_pad = _pad_time(xs.reshape(bs, ts, d), 2 * SUBLANES)
        xs = cross_block(xs_pad, g_cross[i], w_mq_b[i], cache_mem_k[i], cache_mem_v[i],
                         w_mo_b[i])[:, :ts].reshape(bs * ts, d)

        j = i // 2
        if i % 2 == 0:
            wg, wu, wd = bf(w_ff_gate[j]), bf(w_ff_up[j]), bf(w_ff_down[j])
            xp = dense_ffn(xp, g_ffn[i], wg, wu, wd)
            xs = dense_ffn(xs, g_ffn[i], wg, wu, wd)
        else:
            xp, xs = moe_layer([xp, xs], g_ffn[i], w_router[j], bf(w_moe_gate[j]), bf(w_moe_up[j]),
                               bf(w_moe_down[j]), g_out=g_final if i == depth - 1 else None)

    if depth % 2 == 1:
        xp, xs = final_norm(xp, g_final), final_norm(xs, g_final)
    y_prompt = xp.reshape(bp, tp, d)
    y_sample = xs.reshape(bs, ts, d)
    st = lambda name: jnp.stack(outs[name])
    return (y_prompt, y_sample, st("kp"), st("vp"), st("ks"), st("vs"), st("mkp"), st("mvp"),
            st("poolp"), st("pools"), st("scp"), st("scs"), st("cfp"), st("cfs"))
```

```python
import functools

import jax
import jax.numpy as jnp
from jax import lax
from jax.experimental import pallas as pl
from jax.experimental.pallas import tpu as pltpu

EPS = 1e-6
SB_HEADS = 4
MEM_HEADS = 4
POOL_WINDOWS = (2, 4, 8, 16)
TOP_K = 2
LANES = 128
SUBLANES = 8
VMEM_LIMIT_BYTES = 56 * 1024 * 1024
BF16 = jnp.bfloat16
F32 = jnp.float32


def _params(*sem):
    return pltpu.CompilerParams(dimension_semantics=sem, vmem_limit_bytes=VMEM_LIMIT_BYTES)


def _rms(x, g):
    return x * lax.rsqrt(jnp.mean(x * x, axis=-1, keepdims=True) + EPS) * g


def _dot(a, b):
    return jnp.dot(a, b, preferred_element_type=F32)


def _dot_nt(a, b):
    return lax.dot_general(a, b, (((1,), (1,)), ((), ())), preferred_element_type=F32)


def _row_tile(m, want):
    if m <= want:
        return m
    for t in range(want - want % SUBLANES, 0, -SUBLANES):
        if m % t == 0:
            return t
    raise ValueError(f"no aligned row tile for {m}")


def _in_proj_kernel(x_ref, g_ref, w_ref, pa_ref, pb_ref, q_ref, k_ref, v_ref, kb_ref, vb_ref, pd_ref,
                    *, c, qscale, kv_transposed):
    h = _rms(x_ref[...], g_ref[...]).astype(BF16)
    pa_ref[...] = _dot(h, w_ref[:, 0:c])
    pb_ref[...] = _dot(h, w_ref[:, c:4 * c])
    q_ref[...] = (_dot(h, w_ref[:, 4 * c:5 * c]) * qscale).astype(BF16)
    k = _dot(h, w_ref[:, 5 * c:6 * c])
    v = _dot(h, w_ref[:, 6 * c:7 * c])
    if kv_transposed:
        k_ref[0] = k.T
        v_ref[0] = v.T
    else:
        k_ref[...] = k
        v_ref[...] = v
    kb_ref[...] = k.astype(BF16)
    vb_ref[...] = v.astype(BF16)
    pd_ref[...] = _dot(h, w_ref[:, 7 * c:9 * c])


def in_proj(x, g, w_bf16, *, qscale, seq_len=None, tm=512):
    m, d = x.shape
    c = w_bf16.shape[1] // 9
    tm = _row_tile(m if seq_len is None else seq_len, tm)
    row = lambda wd: pl.BlockSpec((tm, wd), lambda i: (i, 0))
    rows = lambda wd, dt: jax.ShapeDtypeStruct((m, wd), dt)
    if seq_len is None:
        kv_spec, kv_shape = row(c), rows(c, F32)
    else:
        tiles = seq_len // tm
        kv_spec = pl.BlockSpec((1, c, tm), lambda i: (i // tiles, 0, i % tiles))
        kv_shape = jax.ShapeDtypeStruct((m // seq_len, c, seq_len), F32)
    return pl.pallas_call(
        functools.partial(_in_proj_kernel, c=c, qscale=qscale, kv_transposed=seq_len is not None),
        grid=(m // tm,),
        in_specs=[pl.BlockSpec((tm, d), lambda i: (i, 0)),
                  pl.BlockSpec((1, d), lambda i: (0, 0)),
                  pl.BlockSpec((d, 9 * c), lambda i: (0, 0))],
        out_specs=[row(c), row(3 * c), row(c), kv_spec, kv_spec, row(c), row(c), row(2 * c)],
        out_shape=[rows(c, F32), rows(3 * c, F32), rows(c, BF16), kv_shape, kv_shape, rows(c, BF16), rows(c, BF16),
                   rows(2 * c, F32)],
        compiler_params=_params("parallel"),
        name="in_proj",
    )(x, g.reshape(1, d), w_bf16)


POOL_HALO = 16
SCONV_HALO = 8
CONF_HALO = 32


def _stage(buf_ref, state_ref, new_rows, halo, tt, first):
    @pl.when(first)
    def _():
        buf_ref[0:halo, :] = state_ref[0]
    buf_ref[halo:halo + tt, :] = new_rows


def _carry_history(buf_ref, halo, tt):
    buf_ref[0:halo, :] = buf_ref[tt:tt + halo, :]


def _causal_conv(buf_ref, w_ref, halo, tt, width):
    full = buf_ref[...]
    rolled = {0: full}
    acc = None
    for j in range(width):
        off = halo - (width - 1) + j
        shift = (-off) % SUBLANES
        if shift not in rolled:
            rolled[shift] = pltpu.roll(full, shift, axis=0)
        start = off + shift
        term = rolled[shift][start:start + tt, :] * w_ref[j:j + 1, :]
        acc = term if acc is None else acc + term
    return acc


def _mixers_kernel(pa_ref, pb_ref, pd_ref, spool_ref, ssc_ref, scf_ref,
                   wpool_ref, pscale_ref, wsc_ref, wcf_ref, bcf_ref, gln_ref, bln_ref,
                   ya_ref, yb_ref, yd_ref, npool_ref, nsc_ref, ncf_ref,
                   bufa_ref, bufb_ref, bufd_ref, *, tt, n_tiles, t_valid, pos0, sc_width, cf_width):
    i = pl.program_id(1)
    first = i == 0
    c = ya_ref.shape[-1]

    u = pa_ref[0]
    _stage(bufa_ref, spool_ref, u, POOL_HALO, tt, first)
    sums, acc, width = {}, bufa_ref[...], 1
    for w in POOL_WINDOWS:
        assert w == 2 * width and w <= POOL_HALO
        acc = acc + pltpu.roll(acc, width, axis=0)
        sums[w], width = acc[POOL_HALO:POOL_HALO + tt, :], w
    lane = lax.broadcasted_iota(jnp.int32, (tt, c), 1)
    grp = lane // (c // len(POOL_WINDOWS))
    win = sums[POOL_WINDOWS[-1]]
    wlen = jnp.full((tt, c), POOL_WINDOWS[-1], jnp.int32)
    for gi in range(len(POOL_WINDOWS) - 2, -1, -1):
        win = jnp.where(grp == gi, sums[POOL_WINDOWS[gi]], win)
        wlen = jnp.where(grp == gi, POOL_WINDOWS[gi], wlen)
    pos = pos0 + i * tt + lax.broadcasted_iota(jnp.int32, (tt, c), 0)
    cnt = jnp.minimum(pos + 1, wlen).astype(F32)
    p = win / cnt - u
    ya_ref[0] = _dot(p.astype(BF16), wpool_ref[...]) * pscale_ref[...]

    pb = pb_ref[0]
    hb, cb, bb = pb[:, 0:c], pb[:, c:2 * c], pb[:, 2 * c:3 * c]
    _stage(bufb_ref, ssc_ref, cb * hb, SCONV_HALO, tt, first)
    yb_ref[0] = bb * _causal_conv(bufb_ref, wsc_ref, SCONV_HALO, tt, sc_width)

    pd = pd_ref[0]
    ad, gd = pd[:, 0:c], pd[:, c:2 * c]
    _stage(bufd_ref, scf_ref, ad * jax.nn.sigmoid(gd), CONF_HALO, tt, first)
    cv = _causal_conv(bufd_ref, wcf_ref, CONF_HALO, tt, cf_width) + bcf_ref[...]
    xc = cv - jnp.mean(cv, axis=-1, keepdims=True)
    ln = xc * lax.rsqrt(jnp.mean(xc * xc, axis=-1, keepdims=True) + EPS) * gln_ref[...] + bln_ref[...]
    yd_ref[0] = ln * jax.nn.sigmoid(ln)

    @pl.when(i == n_tiles - 1)
    def _():
        for out_ref, buf_ref, halo, keep in ((npool_ref, bufa_ref, POOL_HALO, max(POOL_WINDOWS) - 1),
                                             (nsc_ref, bufb_ref, SCONV_HALO, sc_width - 1),
                                             (ncf_ref, bufd_ref, CONF_HALO, cf_width - 1)):
            end = halo + t_valid
            out_ref[0] = buf_ref[end - keep:end, :]

    if n_tiles > 1:
        _carry_history(bufa_ref, POOL_HALO, tt)
        _carry_history(bufb_ref, SCONV_HALO, tt)
        _carry_history(bufd_ref, CONF_HALO, tt)


def _pad_state(state, halo):
    return jnp.pad(state, ((0, 0), (halo - state.shape[1], 0), (0, 0)))


def local_mixers(pa, pb, pd, st_pool, st_sc, st_cf, wpool_bd, pool_scale, w_sc, w_cf, b_cf, g_ln, b_ln,
                 *, t_valid, pos0, tt=512):
    b, t, c = pa.shape
    tt = _row_tile(t, tt)
    n_tiles = t // tt
    assert n_tiles == 1 or tt >= CONF_HALO
    sc_width, cf_width = w_sc.shape[0], w_cf.shape[0]
    keep_pool = max(POOL_WINDOWS) - 1
    tile = lambda wd: pl.BlockSpec((1, tt, wd), lambda bi, i: (bi, i, 0))
    per_b = lambda r, wd: pl.BlockSpec((1, r, wd), lambda bi, i: (bi, 0, 0))
    full2 = lambda a: pl.BlockSpec(a.shape, lambda bi, i: (0, 0))
    row = lambda a: a.reshape(1, c)
    weights = (wpool_bd, row(pool_scale), w_sc, w_cf, row(b_cf), row(g_ln), row(b_ln))
    last_valid = t_valid - (n_tiles - 1) * tt
    return pl.pallas_call(
        functools.partial(_mixers_kernel, tt=tt, n_tiles=n_tiles, t_valid=last_valid, pos0=pos0,
                          sc_width=sc_width, cf_width=cf_width),
        grid=(b, n_tiles),
        in_specs=[tile(c), tile(3 * c), tile(2 * c),
                  per_b(POOL_HALO, c), per_b(SCONV_HALO, c), per_b(CONF_HALO, c)] + [full2(a) for a in weights],
        out_specs=[tile(c), tile(c), tile(c),
                   per_b(keep_pool, c), per_b(sc_width - 1, c), per_b(cf_width - 1, c)],
        out_shape=[jax.ShapeDtypeStruct((b, t, c), F32)] * 3 + [
            jax.ShapeDtypeStruct((b, keep_pool, c), F32),
            jax.ShapeDtypeStruct((b, sc_width - 1, c), F32),
            jax.ShapeDtypeStruct((b, cf_width - 1, c), F32)],
        scratch_shapes=[pltpu.VMEM((POOL_HALO + tt, c), F32),
                        pltpu.VMEM((SCONV_HALO + tt, c), F32),
                        pltpu.VMEM((CONF_HALO + tt, c), F32)],
        compiler_params=_params("parallel", "arbitrary"),
        name="local_mixers",
    )(pa, pb, pd, _pad_state(st_pool, POOL_HALO), _pad_state(st_sc, SCONV_HALO), _pad_state(st_cf, CONF_HALO),
      *weights)


def _suffix_matrix(tk):
    r = lax.broadcasted_iota(jnp.int32, (tk, tk), 0)
    s = lax.broadcasted_iota(jnp.int32, (tk, tk), 1)
    return jnp.where(r > s, 1.0, 0.0).astype(BF16)


LOG2E = 1.4426950408889634
_SIGN_BIT = 0x80000000


def _softplus2(z2, mask):
    neg_abs = lax.bitcast_convert_type(lax.bitcast_convert_type(z2, jnp.uint32) | jnp.uint32(_SIGN_BIT), F32)
    sp = jnp.maximum(z2, 0.0) + jnp.log2(1.0 + jnp.exp2(neg_abs))
    return sp if mask is None else jnp.where(mask, sp, 0.0)


def _later_sum(sp, suffix):
    return _dot(sp.astype(BF16), suffix)


def _sb_weight(z2, sp, later, carry, mask):
    a = jnp.exp2(z2 - sp - later - carry)
    return (a if mask is None else jnp.where(mask, a, 0.0)).astype(BF16)


def _head_select(x, c, heads):
    lane_head = lax.broadcasted_iota(jnp.int32, x.shape, x.ndim - 1) // (c // heads)
    return [jnp.where(lane_head == h, x, jnp.zeros_like(x)) for h in range(heads)]


def _sb_prompt_kernel(bias_ref, q_ref, k_ref, v_ref, o_ref, acc_ref, *, tq):
    qi = pl.program_id(1)
    c = q_ref.shape[-1]
    qh = _head_select(q_ref[0], c, SB_HEADS)
    suffix = _suffix_matrix(tq)
    row = lax.broadcasted_iota(jnp.int32, (tq, tq), 0)
    col = lax.broadcasted_iota(jnp.int32, (tq, tq), 1)
    diag_mask = col < row

    bias2 = [bias_ref[h] * LOG2E for h in range(SB_HEADS)]
    heads = range(SB_HEADS)

    def key_blocks(kbis, carries, mask, first):
        starts = [pl.multiple_of(kbi * tq, tq) for kbi in kbis]
        z2 = [[_dot_nt(qh[h], k_ref[0, pl.ds(s, tq), :]) * LOG2E + bias2[h] for h in heads] for s in starts]
        sp = [[_softplus2(z, mask) for z in zb] for zb in z2]
        later = [[_later_sum(x, suffix) for x in sb] for sb in sp]
        carries = list(carries)
        outs = [None] * SB_HEADS
        for bi, s in enumerate(starts):
            vb = v_ref[0, pl.ds(s, tq), :]
            for h in heads:
                o = _dot(_sb_weight(z2[bi][h], sp[bi][h], later[bi][h], carries[h], mask), vb)
                outs[h] = o if outs[h] is None else outs[h] + o
                carries[h] = carries[h] + jnp.sum(sp[bi][h], axis=1, keepdims=True)
        for h in heads:
            if first:
                acc_ref[h] = outs[h]
            else:
                acc_ref[h] += outs[h]
        return tuple(carries)

    zero = jnp.zeros((tq, 1), F32)
    carries = key_blocks([qi], (zero,) * SB_HEADS, diag_mask, True)
    n_pairs = lax.shift_right_logical(qi, 1)
    carries = lax.fori_loop(
        0, n_pairs, lambda j, cs: key_blocks([qi - 1 - 2 * j, qi - 2 - 2 * j], cs, None, False), carries)

    @pl.when(lax.bitwise_and(qi, 1) == 1)
    def _():
        key_blocks([0], carries, None, False)

    parts = [_head_select(acc_ref[h], c, SB_HEADS)[h] for h in range(SB_HEADS)]
    o_ref[0] = functools.reduce(lambda x, y: x + y, parts)


def sb_prompt(q, k, v, bias, *, tq=256):
    b, t, c = q.shape
    tq = _row_tile(t, tq)
    return pl.pallas_call(
        functools.partial(_sb_prompt_kernel, tq=tq),
        grid_spec=pltpu.PrefetchScalarGridSpec(
            num_scalar_prefetch=1,
            grid=(b, t // tq),
            in_specs=[pl.BlockSpec((1, tq, c), lambda bi, i, *_: (bi, i, 0)),
                      pl.BlockSpec((1, t, c), lambda bi, i, *_: (bi, 0, 0)),
                      pl.BlockSpec((1, t, c), lambda bi, i, *_: (bi, 0, 0))],
            out_specs=pl.BlockSpec((1, tq, c), lambda bi, i, *_: (bi, i, 0)),
            scratch_shapes=[pltpu.VMEM((SB_HEADS, tq, c), F32)]),
        out_shape=jax.ShapeDtypeStruct((b, t, c), F32),
        compiler_params=_params("parallel", "arbitrary"),
        name="sb_prompt",
    )(bias, q, k, v)


SAMPLE_ROWS = 16
PAGES_PER_STEP = 16


def _sb_sample_kernel(pt_ref, bias_ref, q_ref, kn_ref, vn_ref, *rest, n_pages_step):
    k_refs = rest[:n_pages_step]
    v_refs = rest[n_pages_step:2 * n_pages_step]
    o_ref, acc_ref, carry_ref = rest[2 * n_pages_step:]
    j = pl.program_id(1)
    page = kn_ref.shape[-1]
    rows = SB_HEADS * SAMPLE_ROWS
    q_bd = q_ref[0].astype(BF16)
    row = lax.broadcasted_iota(jnp.int32, (rows, 1), 0)
    bias2 = jnp.zeros((rows, 1), F32)
    for h in range(SB_HEADS):
        bias2 = jnp.where(row // SAMPLE_ROWS == h, bias_ref[h] * LOG2E, bias2)
    suffix = _suffix_matrix(page)

    def pages(kt_refs, vt_refs, mask):
        z2 = [_dot(q_bd, kt[...].astype(BF16)) * LOG2E + bias2 for kt in kt_refs]
        sp = [_softplus2(z, mask) for z in z2]
        later = [_later_sum(x, suffix) for x in sp]
        carry = carry_ref[...]
        out = None
        for z, x, lt, vt in zip(z2, sp, later, vt_refs):
            o = _dot_nt(_sb_weight(z, x, lt, carry, mask), vt[...].astype(BF16))
            out = o if out is None else out + o
            carry = carry + jnp.sum(x, axis=1, keepdims=True)
        carry_ref[...] = carry
        return out

    @pl.when(j == 0)
    def _():
        carry_ref[...] = jnp.zeros_like(carry_ref)
        t_new = lax.broadcasted_iota(jnp.int32, (rows, page), 0) % SAMPLE_ROWS
        s_new = lax.broadcasted_iota(jnp.int32, (rows, page), 1)
        acc_ref[...] = pages([kn_ref.at[0]], [vn_ref.at[0]], s_new < t_new)

    acc_ref[...] += pages(k_refs, v_refs, None)

    @pl.when(j == pl.num_programs(1) - 1)
    def _():
        c = acc_ref.shape[-1]
        parts = [_head_select(acc_ref[h * SAMPLE_ROWS:(h + 1) * SAMPLE_ROWS, :], c, SB_HEADS)[h]
                 for h in range(SB_HEADS)]
        o_ref[0] = functools.reduce(lambda x, y: x + y, parts)


def sb_sample(q_bd, kt_new, vt_new, cache_kt, cache_vt, layer, page_table, bias):
    b, rows, c = q_bd.shape
    assert rows == SB_HEADS * SAMPLE_ROWS
    page = cache_kt.shape[-1]
    n_pages = page_table.shape[1]
    pps = PAGES_PER_STEP if n_pages % PAGES_PER_STEP == 0 else 1
    n_steps = n_pages // pps

    def page_spec(p):
        return pl.BlockSpec((None, None, c, page),
                            lambda bi, j, pt, bs: (layer, pt[bi, n_pages - 1 - (j * pps + p)], 0, 0))

    per_b = lambda r, w: pl.BlockSpec((1, r, w), lambda bi, j, *_: (bi, 0, 0))
    return pl.pallas_call(
        functools.partial(_sb_sample_kernel, n_pages_step=pps),
        grid_spec=pltpu.PrefetchScalarGridSpec(
            num_scalar_prefetch=2,
            grid=(b, n_steps),
            in_specs=[per_b(rows, c), per_b(c, page), per_b(c, page)]
                     + [page_spec(p) for p in range(pps)] + [page_spec(p) for p in range(pps)],
            out_specs=per_b(SAMPLE_ROWS, c),
            scratch_shapes=[pltpu.VMEM((rows, c), F32), pltpu.VMEM((rows, 1), F32)]),
        out_shape=jax.ShapeDtypeStruct((b, SAMPLE_ROWS, c), F32),
        compiler_params=_params("parallel", "arbitrary"),
        name="sb_sample",
    )(page_table, bias, q_bd, kt_new, vt_new, *([cache_kt] * pps), *([cache_vt] * pps))


def _out_proj_kernel(x_ref, ya_ref, yb_ref, yc_ref, yd_ref, w_ref, o_ref):
    y = jnp.concatenate([r[...].astype(BF16) for r in (ya_ref, yb_ref, yc_ref, yd_ref)], axis=1)
    o_ref[...] = x_ref[...] + _dot(y, w_ref[...])


def out_proj(x, ya, yb, yc, yd, w_bf16, *, tm=512):
    m, d = x.shape
    c = ya.shape[1]
    tm = _row_tile(m, tm)
    return pl.pallas_call(
        _out_proj_kernel,
        grid=(m // tm,),
        in_specs=[pl.BlockSpec((tm, d), lambda i: (i, 0))] + [pl.BlockSpec((tm, c), lambda i: (i, 0))] * 4
                 + [pl.BlockSpec((d, d), lambda i: (0, 0))],
        out_specs=pl.BlockSpec((tm, d), lambda i: (i, 0)),
        out_shape=jax.ShapeDtypeStruct((m, d), F32),
        compiler_params=_params("parallel"),
        name="out_proj",
    )(x, ya, yb, yc, yd, w_bf16)


def _norm_matmul_kernel(x_ref, g_ref, w_ref, o_ref):
    o_ref[...] = _dot(_rms(x_ref[...], g_ref[...]).astype(BF16), w_ref[...])


def norm_matmul(x, g, w_bf16, *, tm=512):
    m, d = x.shape
    n = w_bf16.shape[1]
    tm = _row_tile(m, tm)
    return pl.pallas_call(
        _norm_matmul_kernel,
        grid=(m // tm,),
        in_specs=[pl.BlockSpec((tm, d), lambda i: (i, 0)),
                  pl.BlockSpec((1, d), lambda i: (0, 0)),
                  pl.BlockSpec((d, n), lambda i: (0, 0))],
        out_specs=pl.BlockSpec((tm, n), lambda i: (i, 0)),
        out_shape=jax.ShapeDtypeStruct((m, n), F32),
        compiler_params=_params("parallel"),
        name="norm_matmul",
    )(x, g.reshape(1, d), w_bf16)


def _mem_attend(q, mk_ref, mv_ref):
    dh = q.shape[-1] // MEM_HEADS
    head = lambda h: slice(h * dh, (h + 1) * dh)
    scores = [_dot_nt(q[:, head(h)], mk_ref[0, :, head(h)].astype(BF16)) for h in range(MEM_HEADS)]
    probs = []
    for s in scores:
        e = jnp.exp(s - jnp.max(s, axis=-1, keepdims=True))
        probs.append((e / jnp.sum(e, axis=-1, keepdims=True)).astype(BF16))
    return [_dot(p, mv_ref[0, :, head(h)].astype(BF16)) for h, p in enumerate(probs)]


def _cross_kernel(x_ref, g_ref, wq_ref, mk_ref, mv_ref, wo_ref, o_ref, *, scale):
    x = x_ref[0]
    q = (_dot(_rms(x, g_ref[...]).astype(BF16), wq_ref[...]) * scale).astype(BF16)
    outs = [o.astype(BF16) for o in _mem_attend(q, mk_ref, mv_ref)]
    o_ref[0] = x + _dot(jnp.concatenate(outs, axis=1), wo_ref[...])


def cross_block(x, g, wq_bf16, mk, mv, wo_bf16, *, tm=512):
    b, t, d = x.shape
    n_mem = mk.shape[1]
    tm = _row_tile(t, tm)
    scale = float((d // MEM_HEADS) ** -0.5)
    const = lambda a: pl.BlockSpec(a.shape, lambda bi, i: (0,) * a.ndim)
    g2 = g.reshape(1, d)
    mem_spec = pl.BlockSpec((1, n_mem, d), lambda bi, i: (bi, 0, 0))
    return pl.pallas_call(
        functools.partial(_cross_kernel, scale=scale),
        grid=(b, t // tm),
        in_specs=[pl.BlockSpec((1, tm, d), lambda bi, i: (bi, i, 0)), const(g2), const(wq_bf16),
                  mem_spec, mem_spec, const(wo_bf16)],
        out_specs=pl.BlockSpec((1, tm, d), lambda bi, i: (bi, i, 0)),
        out_shape=jax.ShapeDtypeStruct((b, t, d), F32),
        compiler_params=_params("parallel", "parallel"),
        name="cross_block",
    )(x, g2, wq_bf16, mk, mv, wo_bf16)


def _mem_attend_kernel(q_ref, mk_ref, mv_ref, o_ref, *, scale):
    q = (q_ref[0] * scale).astype(BF16)
    o_ref[0] = jnp.concatenate(_mem_attend(q, mk_ref, mv_ref), axis=1)


def mem_attend(q, mk, mv):
    b, rows, d = q.shape
    n_mem = mk.shape[1]
    spec = lambda r: pl.BlockSpec((1, r, d), lambda bi: (bi, 0, 0))
    return pl.pallas_call(
        functools.partial(_mem_attend_kernel, scale=float((d // MEM_HEADS) ** -0.5)),
        grid=(b,),
        in_specs=[spec(rows), spec(n_mem), spec(n_mem)],
        out_specs=spec(rows),
        out_shape=jax.ShapeDtypeStruct((b, rows, d), F32),
        compiler_params=_params("parallel"),
        name="mem_attend",
    )(q, mk, mv)


def _matmul_residual_kernel(x_ref, y_ref, w_ref, o_ref):
    o_ref[...] = x_ref[...] + _dot(y_ref[...].astype(BF16), w_ref[...])


def matmul_residual(x, y, w_bf16, *, tm=512):
    m, d = x.shape
    tm = _row_tile(m, tm)
    row = pl.BlockSpec((tm, d), lambda i: (i, 0))
    return pl.pallas_call(
        _matmul_residual_kernel,
        grid=(m // tm,),
        in_specs=[row, row, pl.BlockSpec((d, d), lambda i: (0, 0))],
        out_specs=row,
        out_shape=jax.ShapeDtypeStruct((m, d), F32),
        compiler_params=_params("parallel"),
        name="matmul_residual",
    )(x, y, w_bf16)


def _swiglu_chunk(h_ref, wg_ref, wu_ref, wd_ref, acc_ref):
    h = h_ref[...]
    gate = _dot(h, wg_ref[0])
    up = _dot(h, wu_ref[0])
    act = gate * jax.nn.sigmoid(gate) * up
    acc_ref[...] += _dot(act.astype(BF16), wd_ref[0])


def _ffn_kernel(x_ref, g_ref, wg_ref, wu_ref, wd_ref, o_ref, h_ref, acc_ref):
    j = pl.program_id(1)

    @pl.when(j == 0)
    def _():
        h_ref[...] = _rms(x_ref[...], g_ref[...]).astype(BF16)
        acc_ref[...] = jnp.zeros_like(acc_ref)

    _swiglu_chunk(h_ref, wg_ref, wu_ref, wd_ref, acc_ref)

    @pl.when(j == pl.num_programs(1) - 1)
    def _():
        o_ref[...] = x_ref[...] + acc_ref[...]


def dense_ffn(x, g, wg, wu, wd, *, tm=1024, tf=512):
    m, d = x.shape
    f = wg.shape[1]
    tm, tf = _row_tile(m, tm), _row_tile(f, tf)
    return pl.pallas_call(
        _ffn_kernel,
        grid=(m // tm, f // tf),
        in_specs=[pl.BlockSpec((tm, d), lambda i, j: (i, 0)),
                  pl.BlockSpec((1, d), lambda i, j: (0, 0)),
                  pl.BlockSpec((1, d, tf), lambda i, j: (0, 0, j)),
                  pl.BlockSpec((1, d, tf), lambda i, j: (0, 0, j)),
                  pl.BlockSpec((1, tf, d), lambda i, j: (0, j, 0))],
        out_specs=pl.BlockSpec((tm, d), lambda i, j: (i, 0)),
        scratch_shapes=[pltpu.VMEM((tm, d), BF16), pltpu.VMEM((tm, d), F32)],
        out_shape=jax.ShapeDtypeStruct((m, d), F32),
        compiler_params=_params("parallel", "arbitrary"),
        name="dense_ffn",
    )(x, g.reshape(1, d), wg[None], wu[None], wd[None])


def _moe_ffn_kernel(te_ref, x_ref, wg_ref, wu_ref, wd_ref, o_ref, h_ref, acc_ref):
    j = pl.program_id(1)

    @pl.when(j == 0)
    def _():
        h_ref[...] = x_ref[...].astype(BF16)
        acc_ref[...] = jnp.zeros_like(acc_ref)

    _swiglu_chunk(h_ref, wg_ref, wu_ref, wd_ref, acc_ref)

    @pl.when(j == pl.num_programs(1) - 1)
    def _():
        o_ref[...] = acc_ref[...]


def moe_ffn(x, wg, wu, wd, tile_expert, *, tm, chunks=4):
    n_rows, d = x.shape
    f = wg.shape[2]
    assert n_rows % tm == 0 and f % chunks == 0
    tf = f // chunks
    return pl.pallas_call(
        _moe_ffn_kernel,
        grid_spec=pltpu.PrefetchScalarGridSpec(
            num_scalar_prefetch=1,
            grid=(n_rows // tm, chunks),
            in_specs=[pl.BlockSpec((tm, d), lambda i, j, te: (i, 0)),
                      pl.BlockSpec((1, d, tf), lambda i, j, te: (te[i], 0, j)),
                      pl.BlockSpec((1, d, tf), lambda i, j, te: (te[i], 0, j)),
                      pl.BlockSpec((1, tf, d), lambda i, j, te: (te[i], j, 0))],
            out_specs=pl.BlockSpec((tm, d), lambda i, j, te: (i, 0)),
            scratch_shapes=[pltpu.VMEM((tm, d), BF16), pltpu.VMEM((tm, d), F32)]),
        out_shape=jax.ShapeDtypeStruct((n_rows, d), F32),
        compiler_params=_params("parallel", "arbitrary"),
        name="moe_ffn",
    )(tile_expert, x, wg, wu, wd)


def _router_kernel(x_ref, g_ref, wr_ref, h_ref, gates_ref, sel_ref, *, n_e):
    h = _rms(x_ref[...], g_ref[...])
    h_ref[...] = h
    logits = jnp.dot(h, wr_ref[...], preferred_element_type=F32, precision=lax.Precision.HIGHEST)
    idx = lax.broadcasted_iota(jnp.int32, logits.shape, 1).astype(F32)
    logits = jnp.where(idx < n_e, logits, -jnp.inf)
    m1 = jnp.max(logits, axis=-1, keepdims=True)
    i1 = jnp.min(jnp.where(logits == m1, idx, float(n_e)), axis=-1, keepdims=True)
    rest = jnp.where(idx == i1, -jnp.inf, logits)
    m2 = jnp.max(rest, axis=-1, keepdims=True)
    i2 = jnp.min(jnp.where(rest == m2, idx, float(n_e)), axis=-1, keepdims=True)
    e2 = jnp.exp(m2 - m1)
    g1 = 1.0 / (1.0 + e2)
    g2 = e2 / (1.0 + e2)
    gates_ref[...] = jnp.where(idx == i1, g1, jnp.where(idx == i2, g2, 0.0))
    sel_ref[...] = jnp.where(idx == i1, 1, jnp.where(idx == i2, 1, 0)).astype(jnp.int32)


def router(x, g, w_router, *, tm=512):
    m, d = x.shape
    n_e = w_router.shape[1]
    assert TOP_K <= n_e <= LANES
    tm = _row_tile(m, tm)
    wr = jnp.pad(w_router, ((0, 0), (0, LANES - n_e)))
    h, gates, sel = pl.pallas_call(
        functools.partial(_router_kernel, n_e=n_e),
        grid=(m // tm,),
        in_specs=[pl.BlockSpec((tm, d), lambda i: (i, 0)),
                  pl.BlockSpec((1, d), lambda i: (0, 0)),
                  pl.BlockSpec((d, LANES), lambda i: (0, 0))],
        out_specs=[pl.BlockSpec((tm, d), lambda i: (i, 0)),
                   pl.BlockSpec((tm, LANES), lambda i: (i, 0)),
                   pl.BlockSpec((tm, LANES), lambda i: (i, 0))],
        out_shape=[jax.ShapeDtypeStruct((m, d), F32),
                   jax.ShapeDtypeStruct((m, LANES), F32),
                   jax.ShapeDtypeStruct((m, LANES), jnp.int32)],
        compiler_params=_params("parallel"),
        name="router",
    )(x, g.reshape(1, d), wr)
    return h, gates[:, :n_e], sel[:, :n_e]


def _combine_kernel(x_ref, y1_ref, y2_ref, gt_ref, g_ref, o_ref, *, normed):
    gt = gt_ref[...]
    y = x_ref[...] + gt[:, 0:1] * y1_ref[...] + gt[:, 1:2] * y2_ref[...]
    o_ref[...] = _rms(y, g_ref[...]) if normed else y


def moe_combine(x, y1, y2, gates2, g_out, *, tm=512):
    m, d = x.shape
    tm = _row_tile(m, tm)
    spec = pl.BlockSpec((tm, d), lambda i: (i, 0))
    normed = g_out is not None
    g_row = (g_out if normed else jnp.ones((d,), F32)).reshape(1, d)
    return pl.pallas_call(
        functools.partial(_combine_kernel, normed=normed),
        grid=(m // tm,),
        in_specs=[spec, spec, spec, pl.BlockSpec((tm, TOP_K), lambda i: (i, 0)),
                  pl.BlockSpec((1, d), lambda i: (0, 0))],
        out_specs=spec,
        out_shape=jax.ShapeDtypeStruct((m, d), F32),
        compiler_params=_params("parallel"),
        name="moe_combine",
    )(x, y1, y2, gates2, g_row)


MOE_TILE = 512


def moe_layer(xs, g, w_router, wg, wu, wd, g_out=None):
    n_e = w_router.shape[1]
    tm = MOE_TILE
    routed = [router(x, g, w_router) for x in xs]
    h, gates8, sel8 = (jnp.concatenate(parts, axis=0) for parts in zip(*routed))
    m = h.shape[0]
    lanes = jnp.arange(n_e, dtype=jnp.int32)
    ids = jnp.stack([jnp.min(jnp.where(sel8 > 0, lanes, n_e), axis=-1),
                     jnp.max(jnp.where(sel8 > 0, lanes, -1), axis=-1)], axis=-1)
    gates2 = jnp.take_along_axis(gates8, ids, axis=-1)
    n_tiles = (m * TOP_K + n_e * (tm - 1)) // tm
    rank = jnp.cumsum(sel8, axis=0) - sel8
    counts = rank[-1] + sel8[-1]
    padded = ((counts + tm - 1) // tm) * tm
    pad_end = jnp.cumsum(padded)
    pad_start = pad_end - padded
    pos2 = jnp.take_along_axis(pad_start[None, :] + rank, ids, axis=-1)
    token = jnp.broadcast_to(jnp.arange(m, dtype=jnp.int32)[:, None], (m, TOP_K))
    src_token = jnp.zeros((n_tiles * tm,), jnp.int32).at[pos2.reshape(-1)].set(token.reshape(-1))
    tile_start = jnp.arange(n_tiles, dtype=jnp.int32) * tm
    tile_expert = jnp.minimum(jnp.sum(tile_start[:, None] >= pad_end[None, :], axis=-1), n_e - 1).astype(jnp.int32)
    h_sorted = h.at[src_token].get(mode="promise_in_bounds")
    y_sorted = moe_ffn(h_sorted, wg, wu, wd, tile_expert, tm=tm)
    results, row0 = [], 0
    for x in xs:
        rows = slice(row0, row0 + x.shape[0])
        y1 = y_sorted.at[pos2[rows, 0]].get(mode="promise_in_bounds")
        y2 = y_sorted.at[pos2[rows, 1]].get(mode="promise_in_bounds")
        results.append(moe_combine(x, y1, y2, gates2[rows], g_out))
        row0 += x.shape[0]
    return results


def _final_norm_kernel(x_ref, g_ref, o_ref):
    o_ref[...] = _rms(x_ref[...], g_ref[...])


def final_norm(x, g, *, tm=512):
    m, d = x.shape
    tm = _row_tile(m, tm)
    return pl.pallas_call(
        _final_norm_kernel,
        grid=(m // tm,),
        in_specs=[pl.BlockSpec((tm, d), lambda i: (i, 0)), pl.BlockSpec((1, d), lambda i: (0, 0))],
        out_specs=pl.BlockSpec((tm, d), lambda i: (i, 0)),
        out_shape=jax.ShapeDtypeStruct((m, d), F32),
        compiler_params=_params("parallel"),
        name="final_norm",
    )(x, g.reshape(1, d))


def _block_diag(w):
    g, c, _ = w.shape
    eye = jnp.eye(g, dtype=w.dtype)
    return (eye[:, None, :, None] * w[:, :, None, :]).reshape(g * c, g * c)


def _pad_time(a, t):
    return jnp.pad(a, ((0, 0), (0, t - a.shape[1]), (0, 0)))


def kernel(x_prompt, x_sample, cache_k, cache_v, cache_mem_k, cache_mem_v, state_pool, state_sconv, state_conf, page_table, mem_prompt, g_mix, w_in, w_pool, pool_scale, w_sconv, sb_bias, w_conf, b_conf, g_conf_ln, b_conf_ln, w_out, g_cross, g_mem, w_mq, w_mk, w_mv, w_mo, g_ffn, w_ff_gate, w_ff_up, w_ff_down, w_router, w_moe_gate, w_moe_up, w_moe_down, g_final):
    bp, tp, d = x_prompt.shape
    bs, ts, _ = x_sample.shape
    depth = g_mix.shape[0]
    c = d // 4
    dh = c // SB_HEADS
    page = cache_k.shape[2]
    past = page_table.shape[1] * page
    n_mem = mem_prompt.shape[1]
    qscale = float(dh ** -0.5)
    assert ts <= SAMPLE_ROWS and ts <= page

    bf = lambda a: a.astype(BF16)
    w_in_b, w_out_b, w_mq_b, w_mk_b, w_mv_b, w_mo_b = map(bf, (w_in, w_out, w_mq, w_mk, w_mv, w_mo))
    cache_kt = cache_k.transpose(0, 1, 3, 4, 2).reshape(depth, -1, c, page)
    cache_vt = cache_v.transpose(0, 1, 3, 4, 2).reshape(depth, -1, c, page)
    head_mask = (jnp.arange(c)[None, :] // dh == jnp.arange(SB_HEADS)[:, None]).astype(F32)

    xp = x_prompt.reshape(bp * tp, d)
    xs = x_sample.reshape(bs * ts, d)
    zeros_state = lambda keep: jnp.zeros((bp, keep, c), F32)
    outs = {name: [] for name in ("kp", "vp", "ks", "vs", "mkp", "mvp", "poolp", "pools", "scp", "scs", "cfp", "cfs")}

    for i in range(depth):
        wpool_bd = bf(_block_diag(w_pool[i]))
        mixer_w = (wpool_bd, pool_scale[i], w_sconv[i], w_conf[i], b_conf[i], g_conf_ln[i], b_conf_ln[i])

        pa, pb, q, k_t, v_t, kb, vb, pd = in_proj(xp, g_mix[i], w_in_b[i], qscale=qscale, seq_len=tp)
        r3 = lambda a: a.reshape(bp, tp, a.shape[-1])
        ya, yb, yd, n_pool, n_sc, n_cf = local_mixers(
            r3(pa), r3(pb), r3(pd), zeros_state(max(POOL_WINDOWS) - 1), zeros_state(w_sconv.shape[1] - 1),
            zeros_state(w_conf.shape[1] - 1), *mixer_w, t_valid=tp, pos0=0)
        yc = sb_prompt(r3(q), r3(kb), r3(vb), sb_bias[i])
        f2 = lambda a: a.reshape(bp * tp, c)
        xp = out_proj(xp, f2(ya), f2(yb), f2(yc), f2(yd), w_out_b[i])
        heads_last = lambda a_t: a_t.reshape(bp, SB_HEADS, dh, tp).transpose(0, 3, 1, 2)
        outs["kp"].append(heads_last(k_t))
        outs["vp"].append(heads_last(v_t))
        outs["poolp"].append(n_pool); outs["scp"].append(n_sc); outs["cfp"].append(n_cf)

        mem2 = mem_prompt.reshape(bp * n_mem, d)
        mk_p = norm_matmul(mem2, g_mem[i], w_mk_b[i]).reshape(bp, n_mem, d)
        mv_p = norm_matmul(mem2, g_mem[i], w_mv_b[i]).reshape(bp, n_mem, d)
        outs["mkp"].append(mk_p.reshape(bp, n_mem, MEM_HEADS, d // MEM_HEADS))
        outs["mvp"].append(mv_p.reshape(bp, n_mem, MEM_HEADS, d // MEM_HEADS))
        xp = cross_block(xp.reshape(bp, tp, d), g_cross[i], w_mq_b[i], mk_p, mv_p, w_mo_b[i]).reshape(bp * tp, d)

        pa, pb, q, k, v, kb, vb, pd = in_proj(xs, g_mix[i], w_in_b[i], qscale=qscale)
        r3 = lambda a: _pad_time(a.reshape(bs, ts, a.shape[-1]), SAMPLE_ROWS)
        ya, yb, yd, n_pool, n_sc, n_cf = local_mixers(
            r3(pa), r3(pb), r3(pd), state_pool[i], state_sconv[i], state_conf[i], *mixer_w,
            t_valid=ts, pos0=past)
        q_s = _pad_time(q.astype(F32).reshape(bs, ts, c), SAMPLE_ROWS)
        q_bd = (q_s[:, None] * head_mask[None, :, None]).reshape(bs, SB_HEADS * SAMPLE_ROWS, c)
        new_t = lambda a: jnp.pad(a.reshape(bs, ts, c).transpose(0, 2, 1), ((0, 0), (0, 0), (0, page - ts)))
        yc = sb_sample(q_bd, new_t(k), new_t(v), cache_kt, cache_vt, i, page_table, sb_bias[i])
        yc = yc[:, :ts].reshape(bs * ts, c)
        f2 = lambda a: a[:, :ts].reshape(bs * ts, c)
        xs = out_proj(xs, f2(ya), f2(yb), yc, f2(yd), w_out_b[i])
        outs["ks"].append(k.reshape(bs, ts, SB_HEADS, dh))
        outs["vs"].append(v.reshape(bs, ts, SB_HEADS, dh))
        outs["pools"].append(n_pool); outs["scs"].append(n_sc); outs["cfs"].append(n_cf)

        q_mem = _pad_time(norm_matmul(xs, g_cross[i], w_mq_b[i]).reshape(bs, ts, d), 2 * SUBLANES)
        o_mem = mem_attend(q_mem, cache_mem_k[i].reshape(bs, n_mem, d), cache_mem_v[i].reshape(bs, n_mem, d))
        xs = matmul_residual(xs, o_mem[:, :ts].reshape(bs * ts, d), w_mo_b[i])

        j = i // 2
        if i % 2 == 0:
            wg, wu, wd = bf(w_ff_gate[j]), bf(w_ff_up[j]), bf(w_ff_down[j])
            xp = dense_ffn(xp, g_ffn[i], wg, wu, wd)
            xs = dense_ffn(xs, g_ffn[i], wg, wu, wd)
        else:
            xp, xs = moe_layer([xp, xs], g_ffn[i], w_router[j], bf(w_moe_gate[j]), bf(w_moe_up[j]),
                               bf(w_moe_down[j]), g_out=g_final if i == depth - 1 else None)

    if depth % 2 == 1:
        xp, xs = final_norm(xp, g_final), final_norm(xs, g_final)
    y_prompt = xp.reshape(bp, tp, d)
    y_sample = xs.reshape(bs, ts, d)
    st = lambda name: jnp.stack(outs[name])
    return (y_prompt, y_sample, st("kp"), st("vp"), st("ks"), st("vs"), st("mkp"), st("mvp"),
            st("poolp"), st("pools"), st("scp"), st("scs"), st("cfp"), st("cfs"))
```

```python
import functools

import jax
import jax.numpy as jnp
from jax import lax
from jax.experimental import pallas as pl
from jax.experimental.pallas import tpu as pltpu

EPS = 1e-6
SB_HEADS = 4
MEM_HEADS = 4
POOL_WINDOWS = (2, 4, 8, 16)
TOP_K = 2
LANES = 128
SUBLANES = 8
MXU_WIDTH = 256
VMEM_LIMIT_BYTES = 56 * 1024 * 1024
BF16 = jnp.bfloat16
F32 = jnp.float32


def _params(*sem):
    return pltpu.CompilerParams(dimension_semantics=sem, vmem_limit_bytes=VMEM_LIMIT_BYTES)


def _rms(x, g):
    return x * lax.rsqrt(jnp.mean(x * x, axis=-1, keepdims=True) + EPS) * g


def _dot(a, b):
    return jnp.dot(a, b, preferred_element_type=F32)


def _dot_nt(a, b):
    return lax.dot_general(a, b, (((1,), (1,)), ((), ())), preferred_element_type=F32)


def _row_tile(m, want):
    if m <= want:
        return m
    for t in range(want - want % SUBLANES, 0, -SUBLANES):
        if m % t == 0:
            return t
    raise ValueError(f"no aligned row tile for {m}")


def _in_proj_kernel(x_ref, g_ref, w_ref, pa_ref, pb_ref, q_ref, k_ref, v_ref, kb_ref, vb_ref, pd_ref,
                    *, c, qscale, kv_transposed):
    h = _rms(x_ref[...], g_ref[...]).astype(BF16)
    pa_ref[...] = _dot(h, w_ref[:, 0:c])
    pb_ref[...] = _dot(h, w_ref[:, c:4 * c])
    q_ref[...] = (_dot(h, w_ref[:, 4 * c:5 * c]) * qscale).astype(BF16)
    k = _dot(h, w_ref[:, 5 * c:6 * c])
    v = _dot(h, w_ref[:, 6 * c:7 * c])
    if kv_transposed:
        k_ref[0] = k.T
        v_ref[0] = v.T
    else:
        k_ref[...] = k
        v_ref[...] = v
    kb_ref[...] = k.astype(BF16)
    vb_ref[...] = v.astype(BF16)
    pd_ref[...] = _dot(h, w_ref[:, 7 * c:9 * c])


def in_proj(x, g, w_bf16, *, qscale, seq_len=None, tm=512):
    m, d = x.shape
    c = w_bf16.shape[1] // 9
    tm = _row_tile(m if seq_len is None else seq_len, tm)
    row = lambda wd: pl.BlockSpec((tm, wd), lambda i: (i, 0))
    rows = lambda wd, dt: jax.ShapeDtypeStruct((m, wd), dt)
    if seq_len is None:
        kv_spec, kv_shape = row(c), rows(c, F32)
    else:
        tiles = seq_len // tm
        kv_spec = pl.BlockSpec((1, c, tm), lambda i: (i // tiles, 0, i % tiles))
        kv_shape = jax.ShapeDtypeStruct((m // seq_len, c, seq_len), F32)
    return pl.pallas_call(
        functools.partial(_in_proj_kernel, c=c, qscale=qscale, kv_transposed=seq_len is not None),
        grid=(m // tm,),
        in_specs=[pl.BlockSpec((tm, d), lambda i: (i, 0)),
                  pl.BlockSpec((1, d), lambda i: (0, 0)),
                  pl.BlockSpec((d, 9 * c), lambda i: (0, 0))],
        out_specs=[row(c), row(3 * c), row(c), kv_spec, kv_spec, row(c), row(c), row(2 * c)],
        out_shape=[rows(c, F32), rows(3 * c, F32), rows(c, BF16), kv_shape, kv_shape, rows(c, BF16), rows(c, BF16),
                   rows(2 * c, F32)],
        compiler_params=_params("parallel"),
        name="in_proj",
    )(x, g.reshape(1, d), w_bf16)


POOL_HALO = 16
SCONV_HALO = 8
CONF_HALO = 32


def _stage(buf_ref, state_ref, new_rows, halo, tt, first):
    @pl.when(first)
    def _():
        buf_ref[0:halo, :] = state_ref[0]
    buf_ref[halo:halo + tt, :] = new_rows


def _carry_history(buf_ref, halo, tt):
    buf_ref[0:halo, :] = buf_ref[tt:tt + halo, :]


def _causal_conv(buf_ref, w_ref, halo, tt, width):
    full = buf_ref[...]
    rolled = {0: full}
    acc = None
    for j in range(width):
        off = halo - (width - 1) + j
        shift = (-off) % SUBLANES
        if shift not in rolled:
            rolled[shift] = pltpu.roll(full, shift, axis=0)
        start = off + shift
        term = rolled[shift][start:start + tt, :] * w_ref[j:j + 1, :]
        acc = term if acc is None else acc + term
    return acc


def _mixers_kernel(pa_ref, pb_ref, pd_ref, spool_ref, ssc_ref, scf_ref,
                   wpool_ref, pscale_ref, wsc_ref, wcf_ref, bcf_ref, gln_ref, bln_ref,
                   ya_ref, yb_ref, yd_ref, npool_ref, nsc_ref, ncf_ref,
                   bufa_ref, bufb_ref, bufd_ref, *, tt, n_tiles, t_valid, pos0, sc_width, cf_width):
    i = pl.program_id(1)
    first = i == 0
    c = ya_ref.shape[-1]

    u = pa_ref[0]
    _stage(bufa_ref, spool_ref, u, POOL_HALO, tt, first)
    sums, acc, width = {}, bufa_ref[...], 1
    for w in POOL_WINDOWS:
        assert w == 2 * width and w <= POOL_HALO
        acc = acc + pltpu.roll(acc, width, axis=0)
        sums[w], width = acc[POOL_HALO:POOL_HALO + tt, :], w
    lane = lax.broadcasted_iota(jnp.int32, (tt, c), 1)
    grp = lane // (c // len(POOL_WINDOWS))
    win = sums[POOL_WINDOWS[-1]]
    wlen = jnp.full((tt, c), POOL_WINDOWS[-1], jnp.int32)
    for gi in range(len(POOL_WINDOWS) - 2, -1, -1):
        win = jnp.where(grp == gi, sums[POOL_WINDOWS[gi]], win)
        wlen = jnp.where(grp == gi, POOL_WINDOWS[gi], wlen)
    pos = pos0 + i * tt + lax.broadcasted_iota(jnp.int32, (tt, c), 0)
    cnt = jnp.minimum(pos + 1, wlen).astype(F32)
    p = win / cnt - u
    ya_ref[0] = _dot(p.astype(BF16), wpool_ref[...]) * pscale_ref[...]

    pb = pb_ref[0]
    hb, cb, bb = pb[:, 0:c], pb[:, c:2 * c], pb[:, 2 * c:3 * c]
    _stage(bufb_ref, ssc_ref, cb * hb, SCONV_HALO, tt, first)
    yb_ref[0] = bb * _causal_conv(bufb_ref, wsc_ref, SCONV_HALO, tt, sc_width)

    pd = pd_ref[0]
    ad, gd = pd[:, 0:c], pd[:, c:2 * c]
    _stage(bufd_ref, scf_ref, ad * jax.nn.sigmoid(gd), CONF_HALO, tt, first)
    cv = _causal_conv(bufd_ref, wcf_ref, CONF_HALO, tt, cf_width) + bcf_ref[...]
    xc = cv - jnp.mean(cv, axis=-1, keepdims=True)
    ln = xc * lax.rsqrt(jnp.mean(xc * xc, axis=-1, keepdims=True) + EPS) * gln_ref[...] + bln_ref[...]
    yd_ref[0] = ln * jax.nn.sigmoid(ln)

    @pl.when(i == n_tiles - 1)
    def _():
        for out_ref, buf_ref, halo, keep in ((npool_ref, bufa_ref, POOL_HALO, max(POOL_WINDOWS) - 1),
                                             (nsc_ref, bufb_ref, SCONV_HALO, sc_width - 1),
                                             (ncf_ref, bufd_ref, CONF_HALO, cf_width - 1)):
            end = halo + t_valid
            out_ref[0] = buf_ref[end - keep:end, :]

    if n_tiles > 1:
        _carry_history(bufa_ref, POOL_HALO, tt)
        _carry_history(bufb_ref, SCONV_HALO, tt)
        _carry_history(bufd_ref, CONF_HALO, tt)


def _pad_state(state, halo):
    return jnp.pad(state, ((0, 0), (halo - state.shape[1], 0), (0, 0)))


def local_mixers(pa, pb, pd, st_pool, st_sc, st_cf, wpool_bd, pool_scale, w_sc, w_cf, b_cf, g_ln, b_ln,
                 *, t_valid, pos0, tt=512):
    b, t, c = pa.shape
    tt = _row_tile(t, tt)
    n_tiles = t // tt
    assert n_tiles == 1 or tt >= CONF_HALO
    sc_width, cf_width = w_sc.shape[0], w_cf.shape[0]
    keep_pool = max(POOL_WINDOWS) - 1
    tile = lambda wd: pl.BlockSpec((1, tt, wd), lambda bi, i: (bi, i, 0))
    per_b = lambda r, wd: pl.BlockSpec((1, r, wd), lambda bi, i: (bi, 0, 0))
    full2 = lambda a: pl.BlockSpec(a.shape, lambda bi, i: (0, 0))
    row = lambda a: a.reshape(1, c)
    weights = (wpool_bd, row(pool_scale), w_sc, w_cf, row(b_cf), row(g_ln), row(b_ln))
    last_valid = t_valid - (n_tiles - 1) * tt
    return pl.pallas_call(
        functools.partial(_mixers_kernel, tt=tt, n_tiles=n_tiles, t_valid=last_valid, pos0=pos0,
                          sc_width=sc_width, cf_width=cf_width),
        grid=(b, n_tiles),
        in_specs=[tile(c), tile(3 * c), tile(2 * c),
                  per_b(POOL_HALO, c), per_b(SCONV_HALO, c), per_b(CONF_HALO, c)] + [full2(a) for a in weights],
        out_specs=[tile(c), tile(c), tile(c),
                   per_b(keep_pool, c), per_b(sc_width - 1, c), per_b(cf_width - 1, c)],
        out_shape=[jax.ShapeDtypeStruct((b, t, c), F32)] * 3 + [
            jax.ShapeDtypeStruct((b, keep_pool, c), F32),
            jax.ShapeDtypeStruct((b, sc_width - 1, c), F32),
            jax.ShapeDtypeStruct((b, cf_width - 1, c), F32)],
        scratch_shapes=[pltpu.VMEM((POOL_HALO + tt, c), F32),
                        pltpu.VMEM((SCONV_HALO + tt, c), F32),
                        pltpu.VMEM((CONF_HALO + tt, c), F32)],
        compiler_params=_params("parallel", "arbitrary"),
        name="local_mixers",
    )(pa, pb, pd, _pad_state(st_pool, POOL_HALO), _pad_state(st_sc, SCONV_HALO), _pad_state(st_cf, CONF_HALO),
      *weights)


def _suffix_matrix(tk):
    r = lax.broadcasted_iota(jnp.int32, (tk, tk), 0)
    s = lax.broadcasted_iota(jnp.int32, (tk, tk), 1)
    return jnp.where(r > s, 1.0, 0.0).astype(BF16)


LOG2E = 1.4426950408889634
_SIGN_BIT = 0x80000000


def _softplus2(z2, mask):
    neg_abs = lax.bitcast_convert_type(lax.bitcast_convert_type(z2, jnp.uint32) | jnp.uint32(_SIGN_BIT), F32)
    sp = jnp.maximum(z2, 0.0) + jnp.log2(1.0 + jnp.exp2(neg_abs))
    return sp if mask is None else jnp.where(mask, sp, 0.0)


def _later_sum(sp, suffix):
    return _dot(sp.astype(BF16), suffix)


def _sb_weight(z2, sp, later, carry, mask):
    a = jnp.exp2(z2 - sp - later - carry)
    return (a if mask is None else jnp.where(mask, a, 0.0)).astype(BF16)


def _head_select(x, c, heads):
    lane_head = lax.broadcasted_iota(jnp.int32, x.shape, x.ndim - 1) // (c // heads)
    return [jnp.where(lane_head == h, x, jnp.zeros_like(x)) for h in range(heads)]


def _sb_prompt_kernel(bias_ref, q_ref, k_ref, v_ref, o_ref, acc_ref, *, tq):
    qi = pl.program_id(1)
    c = q_ref.shape[-1]
    qh = _head_select(q_ref[0], c, SB_HEADS)
    suffix = _suffix_matrix(tq)
    row = lax.broadcasted_iota(jnp.int32, (tq, tq), 0)
    col = lax.broadcasted_iota(jnp.int32, (tq, tq), 1)
    diag_mask = col < row

    bias2 = [bias_ref[h] * LOG2E for h in range(SB_HEADS)]
    heads = range(SB_HEADS)

    def key_blocks(kbis, carries, mask, first):
        starts = [pl.multiple_of(kbi * tq, tq) for kbi in kbis]
        z2 = [[_dot_nt(qh[h], k_ref[0, pl.ds(s, tq), :]) * LOG2E + bias2[h] for h in heads] for s in starts]
        sp = [[_softplus2(z, mask) for z in zb] for zb in z2]
        later = [[_later_sum(x, suffix) for x in sb] for sb in sp]
        carries = list(carries)
        outs = [None] * SB_HEADS
        for bi, s in enumerate(starts):
            vb = v_ref[0, pl.ds(s, tq), :]
            for h in heads:
                o = _dot(_sb_weight(z2[bi][h], sp[bi][h], later[bi][h], carries[h], mask), vb)
                outs[h] = o if outs[h] is None else outs[h] + o
                carries[h] = carries[h] + jnp.sum(sp[bi][h], axis=1, keepdims=True)
        for h in heads:
            if first:
                acc_ref[h] = outs[h]
            else:
                acc_ref[h] += outs[h]
        return tuple(carries)

    zero = jnp.zeros((tq, 1), F32)
    carries = key_blocks([qi], (zero,) * SB_HEADS, diag_mask, True)
    n_pairs = lax.shift_right_logical(qi, 1)
    carries = lax.fori_loop(
        0, n_pairs, lambda j, cs: key_blocks([qi - 1 - 2 * j, qi - 2 - 2 * j], cs, None, False), carries)

    @pl.when(lax.bitwise_and(qi, 1) == 1)
    def _():
        key_blocks([0], carries, None, False)

    parts = [_head_select(acc_ref[h], c, SB_HEADS)[h] for h in range(SB_HEADS)]
    o_ref[0] = functools.reduce(lambda x, y: x + y, parts)


def sb_prompt(q, k, v, bias, *, tq=256):
    b, t, c = q.shape
    tq = _row_tile(t, tq)
    return pl.pallas_call(
        functools.partial(_sb_prompt_kernel, tq=tq),
        grid_spec=pltpu.PrefetchScalarGridSpec(
            num_scalar_prefetch=1,
            grid=(b, t // tq),
            in_specs=[pl.BlockSpec((1, tq, c), lambda bi, i, *_: (bi, i, 0)),
                      pl.BlockSpec((1, t, c), lambda bi, i, *_: (bi, 0, 0)),
                      pl.BlockSpec((1, t, c), lambda bi, i, *_: (bi, 0, 0))],
            out_specs=pl.BlockSpec((1, tq, c), lambda bi, i, *_: (bi, i, 0)),
            scratch_shapes=[pltpu.VMEM((SB_HEADS, tq, c), F32)]),
        out_shape=jax.ShapeDtypeStruct((b, t, c), F32),
        compiler_params=_params("parallel", "arbitrary"),
        name="sb_prompt",
    )(bias, q, k, v)


SAMPLE_ROWS = 16
PAGES_PER_STEP = 16


def _sb_sample_kernel(pt_ref, bias_ref, q_ref, kn_ref, vn_ref, *rest, n_pages_step):
    k_refs = rest[:n_pages_step]
    v_refs = rest[n_pages_step:2 * n_pages_step]
    o_ref, acc_ref, carry_ref = rest[2 * n_pages_step:]
    j = pl.program_id(1)
    page = kn_ref.shape[-1]
    rows = SB_HEADS * SAMPLE_ROWS
    q_bd = q_ref[0].astype(BF16)
    row = lax.broadcasted_iota(jnp.int32, (rows, 1), 0)
    bias2 = jnp.zeros((rows, 1), F32)
    for h in range(SB_HEADS):
        bias2 = jnp.where(row // SAMPLE_ROWS == h, bias_ref[h] * LOG2E, bias2)
    suffix = _suffix_matrix(page)

    def pages(kt_refs, vt_refs, mask):
        z2 = [_dot(q_bd, kt[...].astype(BF16)) * LOG2E + bias2 for kt in kt_refs]
        sp = [_softplus2(z, mask) for z in z2]
        later = [_later_sum(x, suffix) for x in sp]
        carry = carry_ref[...]
        out = None
        for z, x, lt, vt in zip(z2, sp, later, vt_refs):
            o = _dot_nt(_sb_weight(z, x, lt, carry, mask), vt[...].astype(BF16))
            out = o if out is None else out + o
            carry = carry + jnp.sum(x, axis=1, keepdims=True)
        carry_ref[...] = carry
        return out

    @pl.when(j == 0)
    def _():
        carry_ref[...] = jnp.zeros_like(carry_ref)
        t_new = lax.broadcasted_iota(jnp.int32, (rows, page), 0) % SAMPLE_ROWS
        s_new = lax.broadcasted_iota(jnp.int32, (rows, page), 1)
        acc_ref[...] = pages([kn_ref.at[0]], [vn_ref.at[0]], s_new < t_new)

    acc_ref[...] += pages(k_refs, v_refs, None)

    @pl.when(j == pl.num_programs(1) - 1)
    def _():
        c = acc_ref.shape[-1]
        parts = [_head_select(acc_ref[h * SAMPLE_ROWS:(h + 1) * SAMPLE_ROWS, :], c, SB_HEADS)[h]
                 for h in range(SB_HEADS)]
        o_ref[0] = functools.reduce(lambda x, y: x + y, parts)


def sb_sample(q_bd, kt_new, vt_new, cache_kt, cache_vt, layer, page_table, bias):
    b, rows, c = q_bd.shape
    assert rows == SB_HEADS * SAMPLE_ROWS
    page = cache_kt.shape[-1]
    n_pages = page_table.shape[1]
    pps = PAGES_PER_STEP if n_pages % PAGES_PER_STEP == 0 else 1
    n_steps = n_pages // pps

    def page_spec(p):
        return pl.BlockSpec((None, None, c, page),
                            lambda bi, j, pt, bs: (layer, pt[bi, n_pages - 1 - (j * pps + p)], 0, 0))

    per_b = lambda r, w: pl.BlockSpec((1, r, w), lambda bi, j, *_: (bi, 0, 0))
    return pl.pallas_call(
        functools.partial(_sb_sample_kernel, n_pages_step=pps),
        grid_spec=pltpu.PrefetchScalarGridSpec(
            num_scalar_prefetch=2,
            grid=(b, n_steps),
            in_specs=[per_b(rows, c), per_b(c, page), per_b(c, page)]
                     + [page_spec(p) for p in range(pps)] + [page_spec(p) for p in range(pps)],
            out_specs=per_b(SAMPLE_ROWS, c),
            scratch_shapes=[pltpu.VMEM((rows, c), F32), pltpu.VMEM((rows, 1), F32)]),
        out_shape=jax.ShapeDtypeStruct((b, SAMPLE_ROWS, c), F32),
        compiler_params=_params("parallel", "arbitrary"),
        name="sb_sample",
    )(page_table, bias, q_bd, kt_new, vt_new, *([cache_kt] * pps), *([cache_vt] * pps))


def _out_proj_kernel(x_ref, ya_ref, yb_ref, yc_ref, yd_ref, w_ref, o_ref):
    y = jnp.concatenate([r[...].astype(BF16) for r in (ya_ref, yb_ref, yc_ref, yd_ref)], axis=1)
    o_ref[...] = x_ref[...] + _dot(y, w_ref[...])


def out_proj(x, ya, yb, yc, yd, w_bf16, *, tm=512):
    m, d = x.shape
    c = ya.shape[1]
    tm = _row_tile(m, tm)
    return pl.pallas_call(
        _out_proj_kernel,
        grid=(m // tm,),
        in_specs=[pl.BlockSpec((tm, d), lambda i: (i, 0))] + [pl.BlockSpec((tm, c), lambda i: (i, 0))] * 4
                 + [pl.BlockSpec((d, d), lambda i: (0, 0))],
        out_specs=pl.BlockSpec((tm, d), lambda i: (i, 0)),
        out_shape=jax.ShapeDtypeStruct((m, d), F32),
        compiler_params=_params("parallel"),
        name="out_proj",
    )(x, ya, yb, yc, yd, w_bf16)


def _norm_matmul_kernel(x_ref, g_ref, w_ref, o_ref):
    o_ref[...] = _dot(_rms(x_ref[...], g_ref[...]).astype(BF16), w_ref[...])


def norm_matmul(x, g, w_bf16, *, tm=512):
    m, d = x.shape
    n = w_bf16.shape[1]
    tm = _row_tile(m, tm)
    return pl.pallas_call(
        _norm_matmul_kernel,
        grid=(m // tm,),
        in_specs=[pl.BlockSpec((tm, d), lambda i: (i, 0)),
                  pl.BlockSpec((1, d), lambda i: (0, 0)),
                  pl.BlockSpec((d, n), lambda i: (0, 0))],
        out_specs=pl.BlockSpec((tm, n), lambda i: (i, 0)),
        out_shape=jax.ShapeDtypeStruct((m, n), F32),
        compiler_params=_params("parallel"),
        name="norm_matmul",
    )(x, g.reshape(1, d), w_bf16)


def _mem_attend(q, mk_ref, mv_ref):
    dh = q.shape[-1] // MEM_HEADS
    head = lambda h: slice(h * dh, (h + 1) * dh)
    scores = [_dot_nt(q[:, head(h)], mk_ref[0, :, head(h)].astype(BF16)) for h in range(MEM_HEADS)]
    probs = []
    for s in scores:
        e = jnp.exp(s - jnp.max(s, axis=-1, keepdims=True))
        probs.append((e / jnp.sum(e, axis=-1, keepdims=True)).astype(BF16))
    return [_dot(p, mv_ref[0, :, head(h)].astype(BF16)) for h, p in enumerate(probs)]


def _cross_kernel(x_ref, g_ref, wq_ref, mk_ref, mv_ref, wo_ref, o_ref, *, scale):
    x = x_ref[0]
    q = (_dot(_rms(x, g_ref[...]).astype(BF16), wq_ref[...]) * scale).astype(BF16)
    outs = [o.astype(BF16) for o in _mem_attend(q, mk_ref, mv_ref)]
    o_ref[0] = x + _dot(jnp.concatenate(outs, axis=1), wo_ref[...])


def cross_block(x, g, wq_bf16, mk, mv, wo_bf16, *, tm=512):
    b, t, d = x.shape
    n_mem = mk.shape[1]
    tm = _row_tile(t, tm)
    scale = float((d // MEM_HEADS) ** -0.5)
    const = lambda a: pl.BlockSpec(a.shape, lambda bi, i: (0,) * a.ndim)
    g2 = g.reshape(1, d)
    mem_spec = pl.BlockSpec((1, n_mem, d), lambda bi, i: (bi, 0, 0))
    return pl.pallas_call(
        functools.partial(_cross_kernel, scale=scale),
        grid=(b, t // tm),
        in_specs=[pl.BlockSpec((1, tm, d), lambda bi, i: (bi, i, 0)), const(g2), const(wq_bf16),
                  mem_spec, mem_spec, const(wo_bf16)],
        out_specs=pl.BlockSpec((1, tm, d), lambda bi, i: (bi, i, 0)),
        out_shape=jax.ShapeDtypeStruct((b, t, d), F32),
        compiler_params=_params("parallel", "parallel"),
        name="cross_block",
    )(x, g2, wq_bf16, mk, mv, wo_bf16)


def _mem_attend_kernel(q_ref, mk_ref, mv_ref, o_ref, *, scale):
    q = (q_ref[0] * scale).astype(BF16)
    o_ref[0] = jnp.concatenate(_mem_attend(q, mk_ref, mv_ref), axis=1)


def mem_attend(q, mk, mv, layer):
    b, rows, d = q.shape
    n_mem = mk.shape[2]
    spec = lambda r: pl.BlockSpec((1, r, d), lambda bi: (bi, 0, 0))
    mem_spec = pl.BlockSpec((None, 1, n_mem, d), lambda bi: (layer, bi, 0, 0))
    return pl.pallas_call(
        functools.partial(_mem_attend_kernel, scale=float((d // MEM_HEADS) ** -0.5)),
        grid=(b,),
        in_specs=[spec(rows), mem_spec, mem_spec],
        out_specs=spec(rows),
        out_shape=jax.ShapeDtypeStruct((b, rows, d), F32),
        compiler_params=_params("parallel"),
        name="mem_attend",
    )(q, mk, mv)


def _matmul_residual_kernel(x_ref, y_ref, w_ref, o_ref):
    o_ref[...] = x_ref[...] + _dot(y_ref[...].astype(BF16), w_ref[...])


def matmul_residual(x, y, w_bf16, *, tm=512):
    m, d = x.shape
    tm = _row_tile(m, tm)
    row = pl.BlockSpec((tm, d), lambda i: (i, 0))
    return pl.pallas_call(
        _matmul_residual_kernel,
        grid=(m // tm,),
        in_specs=[row, row, pl.BlockSpec((d, d), lambda i: (0, 0))],
        out_specs=row,
        out_shape=jax.ShapeDtypeStruct((m, d), F32),
        compiler_params=_params("parallel"),
        name="matmul_residual",
    )(x, y, w_bf16)


def _swiglu_chunk(h_ref, wg_ref, wu_ref, wd_ref, acc_ref):
    h = h_ref[...]
    tf = wg_ref.shape[-1]
    piece = MXU_WIDTH if tf % MXU_WIDTH == 0 else tf
    cols = [slice(p, p + piece) for p in range(0, tf, piece)]
    gates = [_dot(h, wg_ref[0, :, c]) for c in cols]
    ups = [_dot(h, wu_ref[0, :, c]) for c in cols]
    acts = [(g * jax.nn.sigmoid(g) * u).astype(BF16) for g, u in zip(gates, ups)]
    downs = [_dot(a, wd_ref[0, c, :]) for a, c in zip(acts, cols)]
    acc_ref[...] += functools.reduce(lambda x, y: x + y, downs)


def _ffn_kernel(x_ref, g_ref, wg_ref, wu_ref, wd_ref, o_ref, h_ref, acc_ref):
    j = pl.program_id(1)

    @pl.when(j == 0)
    def _():
        h_ref[...] = _rms(x_ref[...], g_ref[...]).astype(BF16)
        acc_ref[...] = jnp.zeros_like(acc_ref)

    _swiglu_chunk(h_ref, wg_ref, wu_ref, wd_ref, acc_ref)

    @pl.when(j == pl.num_programs(1) - 1)
    def _():
        o_ref[...] = x_ref[...] + acc_ref[...]


def dense_ffn(x, g, wg, wu, wd, *, tm=1024, tf=512):
    m, d = x.shape
    f = wg.shape[1]
    tm, tf = _row_tile(m, tm), _row_tile(f, tf)
    return pl.pallas_call(
        _ffn_kernel,
        grid=(m // tm, f // tf),
        in_specs=[pl.BlockSpec((tm, d), lambda i, j: (i, 0)),
                  pl.BlockSpec((1, d), lambda i, j: (0, 0)),
                  pl.BlockSpec((1, d, tf), lambda i, j: (0, 0, j)),
                  pl.BlockSpec((1, d, tf), lambda i, j: (0, 0, j)),
                  pl.BlockSpec((1, tf, d), lambda i, j: (0, j, 0))],
        out_specs=pl.BlockSpec((tm, d), lambda i, j: (i, 0)),
        scratch_shapes=[pltpu.VMEM((tm, d), BF16), pltpu.VMEM((tm, d), F32)],
        out_shape=jax.ShapeDtypeStruct((m, d), F32),
        compiler_params=_params("parallel", "arbitrary"),
        name="dense_ffn",
    )(x, g.reshape(1, d), wg[None], wu[None], wd[None])


def _moe_ffn_kernel(te_ref, first_ref, x_ref, wg_ref, wu_ref, wd_ref, o_ref,
                    wg_res_ref, wu_res_ref, wd_res_ref, h_ref, acc_ref):
    i, j = pl.program_id(0), pl.program_id(1)

    @pl.when(first_ref[i] > 0)
    def _():
        wg_res_ref[j] = wg_ref[0].astype(BF16)
        wu_res_ref[j] = wu_ref[0].astype(BF16)
        wd_res_ref[j] = wd_ref[0].astype(BF16)

    @pl.when(j == 0)
    def _():
        h_ref[...] = x_ref[...].astype(BF16)
        acc_ref[...] = jnp.zeros_like(acc_ref)

    _swiglu_chunk(h_ref, wg_res_ref.at[pl.ds(j, 1)], wu_res_ref.at[pl.ds(j, 1)], wd_res_ref.at[pl.ds(j, 1)], acc_ref)

    @pl.when(j == pl.num_programs(1) - 1)
    def _():
        o_ref[...] = acc_ref[...]


def moe_ffn(x, wg, wu, wd, tile_expert, *, tm, tf=512):
    n_rows, d = x.shape
    f = wg.shape[2]
    tf = _row_tile(f, tf)
    assert n_rows % tm == 0 and tf % LANES == 0
    chunks = f // tf
    n_tiles = n_rows // tm
    changed = tile_expert[1:] != tile_expert[:-1]
    tile_first = jnp.concatenate([jnp.ones((1,), jnp.int32), changed.astype(jnp.int32)])
    chunk = lambda i, j, first: jnp.where(first[i] > 0, j, chunks - 1)
    return pl.pallas_call(
        _moe_ffn_kernel,
        grid_spec=pltpu.PrefetchScalarGridSpec(
            num_scalar_prefetch=2,
            grid=(n_tiles, chunks),
            in_specs=[pl.BlockSpec((tm, d), lambda i, j, te, first: (i, 0)),
                      pl.BlockSpec((1, d, tf), lambda i, j, te, first: (te[i], 0, chunk(i, j, first))),
                      pl.BlockSpec((1, d, tf), lambda i, j, te, first: (te[i], 0, chunk(i, j, first))),
                      pl.BlockSpec((1, tf, d), lambda i, j, te, first: (te[i], chunk(i, j, first), 0))],
            out_specs=pl.BlockSpec((tm, d), lambda i, j, te, first: (i, 0)),
            scratch_shapes=[pltpu.VMEM((chunks, d, tf), BF16), pltpu.VMEM((chunks, d, tf), BF16),
                            pltpu.VMEM((chunks, tf, d), BF16),
                            pltpu.VMEM((tm, d), BF16), pltpu.VMEM((tm, d), F32)]),
        out_shape=jax.ShapeDtypeStruct((n_rows, d), F32),
        compiler_params=_params("arbitrary", "arbitrary"),
        name="moe_ffn",
    )(tile_expert, tile_first, x, wg, wu, wd)


def _router_kernel(x_ref, g_ref, wr_ref, h_ref, gates_ref, sel_ref, *, n_e):
    h = _rms(x_ref[...], g_ref[...])
    h_ref[...] = h
    logits = jnp.dot(h, wr_ref[...], preferred_element_type=F32, precision=lax.Precision.HIGHEST)
    idx = lax.broadcasted_iota(jnp.int32, logits.shape, 1).astype(F32)
    logits = jnp.where(idx < n_e, logits, -jnp.inf)
    m1 = jnp.max(logits, axis=-1, keepdims=True)
    i1 = jnp.min(jnp.where(logits == m1, idx, float(n_e)), axis=-1, keepdims=True)
    rest = jnp.where(idx == i1, -jnp.inf, logits)
    m2 = jnp.max(rest, axis=-1, keepdims=True)
    i2 = jnp.min(jnp.where(rest == m2, idx, float(n_e)), axis=-1, keepdims=True)
    e2 = jnp.exp(m2 - m1)
    g1 = 1.0 / (1.0 + e2)
    g2 = e2 / (1.0 + e2)
    gates_ref[...] = jnp.where(idx == i1, g1, jnp.where(idx == i2, g2, 0.0))
    sel_ref[...] = jnp.where(idx == i1, 1, jnp.where(idx == i2, 1, 0)).astype(jnp.int32)


def router(x, g, w_router, *, tm=512):
    m, d = x.shape
    n_e = w_router.shape[1]
    assert TOP_K <= n_e <= LANES
    tm = _row_tile(m, tm)
    wr = jnp.pad(w_router, ((0, 0), (0, LANES - n_e)))
    h, gates, sel = pl.pallas_call(
        functools.partial(_router_kernel, n_e=n_e),
        grid=(m // tm,),
        in_specs=[pl.BlockSpec((tm, d), lambda i: (i, 0)),
                  pl.BlockSpec((1, d), lambda i: (0, 0)),
                  pl.BlockSpec((d, LANES), lambda i: (0, 0))],
        out_specs=[pl.BlockSpec((tm, d), lambda i: (i, 0)),
                   pl.BlockSpec((tm, LANES), lambda i: (i, 0)),
                   pl.BlockSpec((tm, LANES), lambda i: (i, 0))],
        out_shape=[jax.ShapeDtypeStruct((m, d), F32),
                   jax.ShapeDtypeStruct((m, LANES), F32),
                   jax.ShapeDtypeStruct((m, LANES), jnp.int32)],
        compiler_params=_params("parallel"),
        name="router",
    )(x, g.reshape(1, d), wr)
    return h, gates[:, :n_e], sel[:, :n_e]


def _combine_kernel(x_ref, y1_ref, y2_ref, gt_ref, g_ref, o_ref, *, normed):
    gt = gt_ref[...]
    y = x_ref[...] + gt[:, 0:1] * y1_ref[...] + gt[:, 1:2] * y2_ref[...]
    o_ref[...] = _rms(y, g_ref[...]) if normed else y


def moe_combine(x, y1, y2, gates2, g_out, *, tm=512):
    m, d = x.shape
    tm = _row_tile(m, tm)
    spec = pl.BlockSpec((tm, d), lambda i: (i, 0))
    normed = g_out is not None
    g_row = (g_out if normed else jnp.ones((d,), F32)).reshape(1, d)
    return pl.pallas_call(
        functools.partial(_combine_kernel, normed=normed),
        grid=(m // tm,),
        in_specs=[spec, spec, spec, pl.BlockSpec((tm, TOP_K), lambda i: (i, 0)),
                  pl.BlockSpec((1, d), lambda i: (0, 0))],
        out_specs=spec,
        out_shape=jax.ShapeDtypeStruct((m, d), F32),
        compiler_params=_params("parallel"),
        name="moe_combine",
    )(x, y1, y2, gates2, g_row)


MOE_TILE = 512


def moe_layer(xs, g, w_router, wg, wu, wd, g_out=None):
    n_e = w_router.shape[1]
    tm = MOE_TILE
    routed = [router(x, g, w_router) for x in xs]
    h, gates8, sel8 = (jnp.concatenate(parts, axis=0) for parts in zip(*routed))
    m = h.shape[0]
    lanes = jnp.arange(n_e, dtype=jnp.int32)
    ids = jnp.stack([jnp.min(jnp.where(sel8 > 0, lanes, n_e), axis=-1),
                     jnp.max(jnp.where(sel8 > 0, lanes, -1), axis=-1)], axis=-1)
    gates2 = jnp.take_along_axis(gates8, ids, axis=-1)
    n_tiles = (m * TOP_K + n_e * (tm - 1)) // tm
    rank = jnp.cumsum(sel8, axis=0) - sel8
    counts = rank[-1] + sel8[-1]
    padded = ((counts + tm - 1) // tm) * tm
    pad_end = jnp.cumsum(padded)
    pad_start = pad_end - padded
    pos2 = jnp.take_along_axis(pad_start[None, :] + rank, ids, axis=-1)
    token = jnp.broadcast_to(jnp.arange(m, dtype=jnp.int32)[:, None], (m, TOP_K))
    src_token = jnp.zeros((n_tiles * tm,), jnp.int32).at[pos2.reshape(-1)].set(token.reshape(-1))
    tile_start = jnp.arange(n_tiles, dtype=jnp.int32) * tm
    tile_expert = jnp.minimum(jnp.sum(tile_start[:, None] >= pad_end[None, :], axis=-1), n_e - 1).astype(jnp.int32)
    h_sorted = h.at[src_token].get(mode="promise_in_bounds")
    y_sorted = moe_ffn(h_sorted, wg, wu, wd, tile_expert, tm=tm)
    results, row0 = [], 0
    for x in xs:
        rows = slice(row0, row0 + x.shape[0])
        y1 = y_sorted.at[pos2[rows, 0]].get(mode="promise_in_bounds")
        y2 = y_sorted.at[pos2[rows, 1]].get(mode="promise_in_bounds")
        results.append(moe_combine(x, y1, y2, gates2[rows], g_out))
        row0 += x.shape[0]
    return results


def _final_norm_kernel(x_ref, g_ref, o_ref):
    o_ref[...] = _rms(x_ref[...], g_ref[...])


def final_norm(x, g, *, tm=512):
    m, d = x.shape
    tm = _row_tile(m, tm)
    return pl.pallas_call(
        _final_norm_kernel,
        grid=(m // tm,),
        in_specs=[pl.BlockSpec((tm, d), lambda i: (i, 0)), pl.BlockSpec((1, d), lambda i: (0, 0))],
        out_specs=pl.BlockSpec((tm, d), lambda i: (i, 0)),
        out_shape=jax.ShapeDtypeStruct((m, d), F32),
        compiler_params=_params("parallel"),
        name="final_norm",
    )(x, g.reshape(1, d))


def _block_diag(w):
    g, c, _ = w.shape
    eye = jnp.eye(g, dtype=w.dtype)
    return (eye[:, None, :, None] * w[:, :, None, :]).reshape(g * c, g * c)


def _pad_time(a, t):
    return jnp.pad(a, ((0, 0), (0, t - a.shape[1]), (0, 0)))


def kernel(x_prompt, x_sample, cache_k, cache_v, cache_mem_k, cache_mem_v, state_pool, state_sconv, state_conf, page_table, mem_prompt, g_mix, w_in, w_pool, pool_scale, w_sconv, sb_bias, w_conf, b_conf, g_conf_ln, b_conf_ln, w_out, g_cross, g_mem, w_mq, w_mk, w_mv, w_mo, g_ffn, w_ff_gate, w_ff_up, w_ff_down, w_router, w_moe_gate, w_moe_up, w_moe_down, g_final):
    bp, tp, d = x_prompt.shape
    bs, ts, _ = x_sample.shape
    depth = g_mix.shape[0]
    c = d // 4
    dh = c // SB_HEADS
    page = cache_k.shape[2]
    past = page_table.shape[1] * page
    n_mem = mem_prompt.shape[1]
    qscale = float(dh ** -0.5)
    assert ts <= SAMPLE_ROWS and ts <= page

    bf = lambda a: a.astype(BF16)
    w_in_b, w_out_b, w_mq_b, w_mk_b, w_mv_b, w_mo_b = map(bf, (w_in, w_out, w_mq, w_mk, w_mv, w_mo))
    cache_kt = cache_k.transpose(0, 1, 3, 4, 2).reshape(depth, -1, c, page)
    cache_vt = cache_v.transpose(0, 1, 3, 4, 2).reshape(depth, -1, c, page)
    head_mask = (jnp.arange(c)[None, :] // dh == jnp.arange(SB_HEADS)[:, None]).astype(F32)
    mem_k_all = cache_mem_k.reshape(depth, bs, cache_mem_k.shape[2], d)
    mem_v_all = cache_mem_v.reshape(depth, bs, cache_mem_v.shape[2], d)

    xp = x_prompt.reshape(bp * tp, d)
    xs = x_sample.reshape(bs * ts, d)
    zeros_state = lambda keep: jnp.zeros((bp, keep, c), F32)
    outs = {name: [] for name in ("kp", "vp", "ks", "vs", "mkp", "mvp", "poolp", "pools", "scp", "scs", "cfp", "cfs")}

    for i in range(depth):
        wpool_bd = bf(_block_diag(w_pool[i]))
        mixer_w = (wpool_bd, pool_scale[i], w_sconv[i], w_conf[i], b_conf[i], g_conf_ln[i], b_conf_ln[i])

        pa, pb, q, k_t, v_t, kb, vb, pd = in_proj(xp, g_mix[i], w_in_b[i], qscale=qscale, seq_len=tp)
        r3 = lambda a: a.reshape(bp, tp, a.shape[-1])
        ya, yb, yd, n_pool, n_sc, n_cf = local_mixers(
            r3(pa), r3(pb), r3(pd), zeros_state(max(POOL_WINDOWS) - 1), zeros_state(w_sconv.shape[1] - 1),
            zeros_state(w_conf.shape[1] - 1), *mixer_w, t_valid=tp, pos0=0)
        yc = sb_prompt(r3(q), r3(kb), r3(vb), sb_bias[i])
        f2 = lambda a: a.reshape(bp * tp, c)
        xp = out_proj(xp, f2(ya), f2(yb), f2(yc), f2(yd), w_out_b[i])
        heads_last = lambda a_t: a_t.reshape(bp, SB_HEADS, dh, tp).transpose(0, 3, 1, 2)
        outs["kp"].append(heads_last(k_t))
        outs["vp"].append(heads_last(v_t))
        outs["poolp"].append(n_pool); outs["scp"].append(n_sc); outs["cfp"].append(n_cf)

        mem2 = mem_prompt.reshape(bp * n_mem, d)
        mk_p = norm_matmul(mem2, g_mem[i], w_mk_b[i]).reshape(bp, n_mem, d)
        mv_p = norm_matmul(mem2, g_mem[i], w_mv_b[i]).reshape(bp, n_mem, d)
        outs["mkp"].append(mk_p.reshape(bp, n_mem, MEM_HEADS, d // MEM_HEADS))
        outs["mvp"].append(mv_p.reshape(bp, n_mem, MEM_HEADS, d // MEM_HEADS))
        xp = cross_block(xp.reshape(bp, tp, d), g_cross[i], w_mq_b[i], mk_p, mv_p, w_mo_b[i]).reshape(bp * tp, d)

        pa, pb, q, k, v, kb, vb, pd = in_proj(xs, g_mix[i], w_in_b[i], qscale=qscale)
        r3 = lambda a: _pad_time(a.reshape(bs, ts, a.shape[-1]), SAMPLE_ROWS)
        ya, yb, yd, n_pool, n_sc, n_cf = local_mixers(
            r3(pa), r3(pb), r3(pd), state_pool[i], state_sconv[i], state_conf[i], *mixer_w,
            t_valid=ts, pos0=past)
        q_s = _pad_time(q.astype(F32).reshape(bs, ts, c), SAMPLE_ROWS)
        q_bd = (q_s[:, None] * head_mask[None, :, None]).reshape(bs, SB_HEADS * SAMPLE_ROWS, c)
        new_t = lambda a: jnp.pad(a.reshape(bs, ts, c).transpose(0, 2, 1), ((0, 0), (0, 0), (0, page - ts)))
        yc = sb_sample(q_bd, new_t(k), new_t(v), cache_kt, cache_vt, i, page_table, sb_bias[i])
        yc = yc[:, :ts].reshape(bs * ts, c)
        f2 = lambda a: a[:, :ts].reshape(bs * ts, c)
        xs = out_proj(xs, f2(ya), f2(yb), yc, f2(yd), w_out_b[i])
        outs["ks"].append(k.reshape(bs, ts, SB_HEADS, dh))
        outs["vs"].append(v.reshape(bs, ts, SB_HEADS, dh))
        outs["pools"].append(n_pool); outs["scs"].append(n_sc); outs["cfs"].append(n_cf)

        q_mem = _pad_time(norm_matmul(xs, g_cross[i], w_mq_b[i]).reshape(bs, ts, d), 2 * SUBLANES)
        o_mem = mem_attend(q_mem, mem_k_all, mem_v_all, i)
        xs = matmul_residual(xs, o_mem[:, :ts].reshape(bs * ts, d), w_mo_b[i])

        j = i // 2
        if i % 2 == 0:
            wg, wu, wd = bf(w_ff_gate[j]), bf(w_ff_up[j]), bf(w_ff_down[j])
            xp = dense_ffn(xp, g_ffn[i], wg, wu, wd)
            xs = dense_ffn(xs, g_ffn[i], wg, wu, wd)
        else:
            xp, xs = moe_layer([xp, xs], g_ffn[i], w_router[j], w_moe_gate[j], w_moe_up[j], w_moe_down[j],
                               g_out=g_final if i == depth - 1 else None)

    if depth % 2 == 1:
        xp, xs = final_norm(xp, g_final), final_norm(xs, g_final)
    y_prompt = xp.reshape(bp, tp, d)
    y_sample = xs.reshape(bs, ts, d)
    st = lambda name: jnp.stack(outs[name])
    return (y_prompt, y_sample, st("kp"), st("vp"), st("ks"), st("vs"), st("mkp"), st("mvp"),
            st("poolp"), st("pools"), st("scp"), st("scs"), st("cfp"), st("cfs"))
```

```python
import functools

import jax
import jax.numpy as jnp
from jax import lax
from jax.experimental import pallas as pl
from jax.experimental.pallas import tpu as pltpu

EPS = 1e-6
SB_HEADS = 4
MEM_HEADS = 4
POOL_WINDOWS = (2, 4, 8, 16)
TOP_K = 2
LANES = 128
SUBLANES = 8
MXU_WIDTH = 256
VMEM_LIMIT_BYTES = 56 * 1024 * 1024
BF16 = jnp.bfloat16
F32 = jnp.float32


def _params(*sem):
    return pltpu.CompilerParams(dimension_semantics=sem, vmem_limit_bytes=VMEM_LIMIT_BYTES)


def _rms(x, g):
    return x * lax.rsqrt(jnp.mean(x * x, axis=-1, keepdims=True) + EPS) * g


def _dot(a, b):
    return jnp.dot(a, b, preferred_element_type=F32)


def _dot_nt(a, b):
    return lax.dot_general(a, b, (((1,), (1,)), ((), ())), preferred_element_type=F32)


def _row_tile(m, want):
    if m <= want:
        return m
    for t in range(want - want % SUBLANES, 0, -SUBLANES):
        if m % t == 0:
            return t
    raise ValueError(f"no aligned row tile for {m}")


def _in_proj_kernel(x_ref, g_ref, w_ref, pa_ref, pb_ref, q_ref, k_ref, v_ref, kb_ref, vb_ref, pd_ref,
                    *, c, qscale, kv_transposed):
    h = _rms(x_ref[...], g_ref[...]).astype(BF16)
    pa_ref[...] = _dot(h, w_ref[:, 0:c])
    pb_ref[...] = _dot(h, w_ref[:, c:4 * c])
    q_ref[...] = (_dot(h, w_ref[:, 4 * c:5 * c]) * qscale).astype(BF16)
    k = _dot(h, w_ref[:, 5 * c:6 * c])
    v = _dot(h, w_ref[:, 6 * c:7 * c])
    if kv_transposed:
        k_ref[0] = k.T
        v_ref[0] = v.T
    else:
        k_ref[...] = k
        v_ref[...] = v
    kb_ref[...] = k.astype(BF16)
    vb_ref[...] = v.astype(BF16)
    pd_ref[...] = _dot(h, w_ref[:, 7 * c:9 * c])


def in_proj(x, g, w_bf16, *, qscale, seq_len=None, tm=512):
    m, d = x.shape
    c = w_bf16.shape[1] // 9
    tm = _row_tile(m if seq_len is None else seq_len, tm)
    row = lambda wd: pl.BlockSpec((tm, wd), lambda i: (i, 0))
    rows = lambda wd, dt: jax.ShapeDtypeStruct((m, wd), dt)
    if seq_len is None:
        kv_spec, kv_shape = row(c), rows(c, F32)
    else:
        tiles = seq_len // tm
        kv_spec = pl.BlockSpec((1, c, tm), lambda i: (i // tiles, 0, i % tiles))
        kv_shape = jax.ShapeDtypeStruct((m // seq_len, c, seq_len), F32)
    return pl.pallas_call(
        functools.partial(_in_proj_kernel, c=c, qscale=qscale, kv_transposed=seq_len is not None),
        grid=(m // tm,),
        in_specs=[pl.BlockSpec((tm, d), lambda i: (i, 0)),
                  pl.BlockSpec((1, d), lambda i: (0, 0)),
                  pl.BlockSpec((d, 9 * c), lambda i: (0, 0))],
        out_specs=[row(c), row(3 * c), row(c), kv_spec, kv_spec, row(c), row(c), row(2 * c)],
        out_shape=[rows(c, F32), rows(3 * c, F32), rows(c, BF16), kv_shape, kv_shape, rows(c, BF16), rows(c, BF16),
                   rows(2 * c, F32)],
        compiler_params=_params("parallel"),
        name="in_proj",
    )(x, g.reshape(1, d), w_bf16)


POOL_HALO = 16
SCONV_HALO = 8
CONF_HALO = 32


def _stage(buf_ref, state_ref, new_rows, halo, tt, first):
    @pl.when(first)
    def _():
        buf_ref[0:halo, :] = state_ref[0]
    buf_ref[halo:halo + tt, :] = new_rows


def _carry_history(buf_ref, halo, tt):
    buf_ref[0:halo, :] = buf_ref[tt:tt + halo, :]


def _causal_conv(buf_ref, w_ref, halo, tt, width):
    full = buf_ref[...]
    rolled = {0: full}
    acc = None
    for j in range(width):
        off = halo - (width - 1) + j
        shift = (-off) % SUBLANES
        if shift not in rolled:
            rolled[shift] = pltpu.roll(full, shift, axis=0)
        start = off + shift
        term = rolled[shift][start:start + tt, :] * w_ref[j:j + 1, :]
        acc = term if acc is None else acc + term
    return acc


def _mixers_kernel(pa_ref, pb_ref, pd_ref, spool_ref, ssc_ref, scf_ref,
                   wpool_ref, pscale_ref, wsc_ref, wcf_ref, bcf_ref, gln_ref, bln_ref,
                   ya_ref, yb_ref, yd_ref, npool_ref, nsc_ref, ncf_ref,
                   bufa_ref, bufb_ref, bufd_ref, *, tt, n_tiles, t_valid, pos0, sc_width, cf_width):
    i = pl.program_id(1)
    first = i == 0
    c = ya_ref.shape[-1]

    u = pa_ref[0]
    _stage(bufa_ref, spool_ref, u, POOL_HALO, tt, first)
    sums, acc, width = {}, bufa_ref[...], 1
    for w in POOL_WINDOWS:
        assert w == 2 * width and w <= POOL_HALO
        acc = acc + pltpu.roll(acc, width, axis=0)
        sums[w], width = acc[POOL_HALO:POOL_HALO + tt, :], w
    lane = lax.broadcasted_iota(jnp.int32, (tt, c), 1)
    grp = lane // (c // len(POOL_WINDOWS))
    win = sums[POOL_WINDOWS[-1]]
    wlen = jnp.full((tt, c), POOL_WINDOWS[-1], jnp.int32)
    for gi in range(len(POOL_WINDOWS) - 2, -1, -1):
        win = jnp.where(grp == gi, sums[POOL_WINDOWS[gi]], win)
        wlen = jnp.where(grp == gi, POOL_WINDOWS[gi], wlen)
    pos = pos0 + i * tt + lax.broadcasted_iota(jnp.int32, (tt, c), 0)
    cnt = jnp.minimum(pos + 1, wlen).astype(F32)
    p = win / cnt - u
    ya_ref[0] = _dot(p.astype(BF16), wpool_ref[...]) * pscale_ref[...]

    pb = pb_ref[0]
    hb, cb, bb = pb[:, 0:c], pb[:, c:2 * c], pb[:, 2 * c:3 * c]
    _stage(bufb_ref, ssc_ref, cb * hb, SCONV_HALO, tt, first)
    yb_ref[0] = bb * _causal_conv(bufb_ref, wsc_ref, SCONV_HALO, tt, sc_width)

    pd = pd_ref[0]
    ad, gd = pd[:, 0:c], pd[:, c:2 * c]
    _stage(bufd_ref, scf_ref, ad * jax.nn.sigmoid(gd), CONF_HALO, tt, first)
    cv = _causal_conv(bufd_ref, wcf_ref, CONF_HALO, tt, cf_width) + bcf_ref[...]
    xc = cv - jnp.mean(cv, axis=-1, keepdims=True)
    ln = xc * lax.rsqrt(jnp.mean(xc * xc, axis=-1, keepdims=True) + EPS) * gln_ref[...] + bln_ref[...]
    yd_ref[0] = ln * jax.nn.sigmoid(ln)

    @pl.when(i == n_tiles - 1)
    def _():
        for out_ref, buf_ref, halo, keep in ((npool_ref, bufa_ref, POOL_HALO, max(POOL_WINDOWS) - 1),
                                             (nsc_ref, bufb_ref, SCONV_HALO, sc_width - 1),
                                             (ncf_ref, bufd_ref, CONF_HALO, cf_width - 1)):
            end = halo + t_valid
            out_ref[0] = buf_ref[end - keep:end, :]

    if n_tiles > 1:
        _carry_history(bufa_ref, POOL_HALO, tt)
        _carry_history(bufb_ref, SCONV_HALO, tt)
        _carry_history(bufd_ref, CONF_HALO, tt)


def _pad_state(state, halo):
    return jnp.pad(state, ((0, 0), (halo - state.shape[1], 0), (0, 0)))


def local_mixers(pa, pb, pd, st_pool, st_sc, st_cf, wpool_bd, pool_scale, w_sc, w_cf, b_cf, g_ln, b_ln,
                 *, t_valid, pos0, tt=512):
    b, t, c = pa.shape
    tt = _row_tile(t, tt)
    n_tiles = t // tt
    assert n_tiles == 1 or tt >= CONF_HALO
    sc_width, cf_width = w_sc.shape[0], w_cf.shape[0]
    keep_pool = max(POOL_WINDOWS) - 1
    tile = lambda wd: pl.BlockSpec((1, tt, wd), lambda bi, i: (bi, i, 0))
    per_b = lambda r, wd: pl.BlockSpec((1, r, wd), lambda bi, i: (bi, 0, 0))
    full2 = lambda a: pl.BlockSpec(a.shape, lambda bi, i: (0, 0))
    row = lambda a: a.reshape(1, c)
    weights = (wpool_bd, row(pool_scale), w_sc, w_cf, row(b_cf), row(g_ln), row(b_ln))
    last_valid = t_valid - (n_tiles - 1) * tt
    return pl.pallas_call(
        functools.partial(_mixers_kernel, tt=tt, n_tiles=n_tiles, t_valid=last_valid, pos0=pos0,
                          sc_width=sc_width, cf_width=cf_width),
        grid=(b, n_tiles),
        in_specs=[tile(c), tile(3 * c), tile(2 * c),
                  per_b(POOL_HALO, c), per_b(SCONV_HALO, c), per_b(CONF_HALO, c)] + [full2(a) for a in weights],
        out_specs=[tile(c), tile(c), tile(c),
                   per_b(keep_pool, c), per_b(sc_width - 1, c), per_b(cf_width - 1, c)],
        out_shape=[jax.ShapeDtypeStruct((b, t, c), F32)] * 3 + [
            jax.ShapeDtypeStruct((b, keep_pool, c), F32),
            jax.ShapeDtypeStruct((b, sc_width - 1, c), F32),
            jax.ShapeDtypeStruct((b, cf_width - 1, c), F32)],
        scratch_shapes=[pltpu.VMEM((POOL_HALO + tt, c), F32),
                        pltpu.VMEM((SCONV_HALO + tt, c), F32),
                        pltpu.VMEM((CONF_HALO + tt, c), F32)],
        compiler_params=_params("parallel", "arbitrary"),
        name="local_mixers",
    )(pa, pb, pd, _pad_state(st_pool, POOL_HALO), _pad_state(st_sc, SCONV_HALO), _pad_state(st_cf, CONF_HALO),
      *weights)


def _suffix_matrix(tk):
    r = lax.broadcasted_iota(jnp.int32, (tk, tk), 0)
    s = lax.broadcasted_iota(jnp.int32, (tk, tk), 1)
    return jnp.where(r > s, 1.0, 0.0).astype(BF16)


LOG2E = 1.4426950408889634
_SIGN_BIT = 0x80000000


def _softplus2(z2, mask):
    neg_abs = lax.bitcast_convert_type(lax.bitcast_convert_type(z2, jnp.uint32) | jnp.uint32(_SIGN_BIT), F32)
    sp = jnp.maximum(z2, 0.0) + jnp.log2(1.0 + jnp.exp2(neg_abs))
    return sp if mask is None else jnp.where(mask, sp, 0.0)


def _later_sum(sp, suffix):
    return _dot(sp.astype(BF16), suffix)


def _sb_weight(z2, sp, later, carry, mask):
    a = jnp.exp2(z2 - sp - later - carry)
    return (a if mask is None else jnp.where(mask, a, 0.0)).astype(BF16)


def _head_select(x, c, heads):
    lane_head = lax.broadcasted_iota(jnp.int32, x.shape, x.ndim - 1) // (c // heads)
    return [jnp.where(lane_head == h, x, jnp.zeros_like(x)) for h in range(heads)]


SB_BLOCKS_PER_ITERATION = 4


def _sb_prompt_kernel(bias_ref, q_ref, k_ref, v_ref, o_ref, acc_ref, *, tq):
    qi = pl.program_id(1)
    c = q_ref.shape[-1]
    qh = _head_select(q_ref[0], c, SB_HEADS)
    suffix = _suffix_matrix(tq)
    row = lax.broadcasted_iota(jnp.int32, (tq, tq), 0)
    col = lax.broadcasted_iota(jnp.int32, (tq, tq), 1)
    diag_mask = col < row

    bias2 = [bias_ref[h] * LOG2E for h in range(SB_HEADS)]
    heads = range(SB_HEADS)

    def key_blocks(kbis, carries, mask, first):
        starts = [pl.multiple_of(kbi * tq, tq) for kbi in kbis]
        z2 = [[_dot_nt(qh[h], k_ref[0, pl.ds(s, tq), :]) * LOG2E + bias2[h] for h in heads] for s in starts]
        sp = [[_softplus2(z, mask) for z in zb] for zb in z2]
        later = [[_later_sum(x, suffix) for x in sb] for sb in sp]
        carries = list(carries)
        outs = [None] * SB_HEADS
        for bi, s in enumerate(starts):
            vb = v_ref[0, pl.ds(s, tq), :]
            for h in heads:
                o = _dot(_sb_weight(z2[bi][h], sp[bi][h], later[bi][h], carries[h], mask), vb)
                outs[h] = o if outs[h] is None else outs[h] + o
                carries[h] = carries[h] + jnp.sum(sp[bi][h], axis=1, keepdims=True)
        for h in heads:
            if first:
                acc_ref[h] = outs[h]
            else:
                acc_ref[h] += outs[h]
        return tuple(carries)

    zero = jnp.zeros((tq, 1), F32)
    carries = key_blocks([qi], (zero,) * SB_HEADS, diag_mask, True)
    group = SB_BLOCKS_PER_ITERATION
    assert group & (group - 1) == 0
    n_groups = lax.shift_right_logical(qi, group.bit_length() - 1)
    carries = lax.fori_loop(
        0, n_groups, lambda j, cs: key_blocks([qi - 1 - group * j - u for u in range(group)], cs, None, False),
        carries)
    rest = lax.bitwise_and(qi, group - 1)
    lax.fori_loop(0, rest, lambda j, cs: key_blocks([rest - 1 - j], cs, None, False), carries)

    parts = [_head_select(acc_ref[h], c, SB_HEADS)[h] for h in range(SB_HEADS)]
    o_ref[0] = functools.reduce(lambda x, y: x + y, parts)


def sb_prompt(q, k, v, bias, *, tq=256):
    b, t, c = q.shape
    tq = _row_tile(t, tq)
    return pl.pallas_call(
        functools.partial(_sb_prompt_kernel, tq=tq),
        grid_spec=pltpu.PrefetchScalarGridSpec(
            num_scalar_prefetch=1,
            grid=(b, t // tq),
            in_specs=[pl.BlockSpec((1, tq, c), lambda bi, i, *_: (bi, i, 0)),
                      pl.BlockSpec((1, t, c), lambda bi, i, *_: (bi, 0, 0)),
                      pl.BlockSpec((1, t, c), lambda bi, i, *_: (bi, 0, 0))],
            out_specs=pl.BlockSpec((1, tq, c), lambda bi, i, *_: (bi, i, 0)),
            scratch_shapes=[pltpu.VMEM((SB_HEADS, tq, c), F32)]),
        out_shape=jax.ShapeDtypeStruct((b, t, c), F32),
        compiler_params=_params("parallel", "arbitrary"),
        name="sb_prompt",
    )(bias, q, k, v)


SAMPLE_ROWS = 16
PAGES_PER_STEP = 16


def _sb_sample_kernel(pt_ref, bias_ref, q_ref, kn_ref, vn_ref, *rest, n_pages_step):
    k_refs = rest[:n_pages_step]
    v_refs = rest[n_pages_step:2 * n_pages_step]
    o_ref, acc_ref, carry_ref = rest[2 * n_pages_step:]
    j = pl.program_id(1)
    page = kn_ref.shape[-1]
    rows = SB_HEADS * SAMPLE_ROWS
    q_bd = q_ref[0].astype(BF16)
    row = lax.broadcasted_iota(jnp.int32, (rows, 1), 0)
    bias2 = jnp.zeros((rows, 1), F32)
    for h in range(SB_HEADS):
        bias2 = jnp.where(row // SAMPLE_ROWS == h, bias_ref[h] * LOG2E, bias2)
    suffix = _suffix_matrix(page)

    def pages(kt_refs, vt_refs, mask):
        z2 = [_dot(q_bd, kt[...].astype(BF16)) * LOG2E + bias2 for kt in kt_refs]
        sp = [_softplus2(z, mask) for z in z2]
        later = [_later_sum(x, suffix) for x in sp]
        carry = carry_ref[...]
        out = None
        for z, x, lt, vt in zip(z2, sp, later, vt_refs):
            o = _dot_nt(_sb_weight(z, x, lt, carry, mask), vt[...].astype(BF16))
            out = o if out is None else out + o
            carry = carry + jnp.sum(x, axis=1, keepdims=True)
        carry_ref[...] = carry
        return out

    @pl.when(j == 0)
    def _():
        carry_ref[...] = jnp.zeros_like(carry_ref)
        t_new = lax.broadcasted_iota(jnp.int32, (rows, page), 0) % SAMPLE_ROWS
        s_new = lax.broadcasted_iota(jnp.int32, (rows, page), 1)
        acc_ref[...] = pages([kn_ref.at[0]], [vn_ref.at[0]], s_new < t_new)

    acc_ref[...] += pages(k_refs, v_refs, None)

    @pl.when(j == pl.num_programs(1) - 1)
    def _():
        c = acc_ref.shape[-1]
        parts = [_head_select(acc_ref[h * SAMPLE_ROWS:(h + 1) * SAMPLE_ROWS, :], c, SB_HEADS)[h]
                 for h in range(SB_HEADS)]
        o_ref[0] = functools.reduce(lambda x, y: x + y, parts)


def sb_sample(q_bd, kt_new, vt_new, cache_kt, cache_vt, layer, page_table, bias):
    b, rows, c = q_bd.shape
    assert rows == SB_HEADS * SAMPLE_ROWS
    page = cache_kt.shape[-1]
    n_pages = page_table.shape[1]
    pps = PAGES_PER_STEP if n_pages % PAGES_PER_STEP == 0 else 1
    n_steps = n_pages // pps

    def page_spec(p):
        return pl.BlockSpec((None, None, c, page),
                            lambda bi, j, pt, bs: (layer, pt[bi, n_pages - 1 - (j * pps + p)], 0, 0))

    per_b = lambda r, w: pl.BlockSpec((1, r, w), lambda bi, j, *_: (bi, 0, 0))
    return pl.pallas_call(
        functools.partial(_sb_sample_kernel, n_pages_step=pps),
        grid_spec=pltpu.PrefetchScalarGridSpec(
            num_scalar_prefetch=2,
            grid=(b, n_steps),
            in_specs=[per_b(rows, c), per_b(c, page), per_b(c, page)]
                     + [page_spec(p) for p in range(pps)] + [page_spec(p) for p in range(pps)],
            out_specs=per_b(SAMPLE_ROWS, c),
            scratch_shapes=[pltpu.VMEM((rows, c), F32), pltpu.VMEM((rows, 1), F32)]),
        out_shape=jax.ShapeDtypeStruct((b, SAMPLE_ROWS, c), F32),
        compiler_params=_params("parallel", "arbitrary"),
        name="sb_sample",
    )(page_table, bias, q_bd, kt_new, vt_new, *([cache_kt] * pps), *([cache_vt] * pps))


def _out_proj_kernel(x_ref, ya_ref, yb_ref, yc_ref, yd_ref, w_ref, o_ref):
    y = jnp.concatenate([r[...].astype(BF16) for r in (ya_ref, yb_ref, yc_ref, yd_ref)], axis=1)
    o_ref[...] = x_ref[...] + _dot(y, w_ref[...])


def out_proj(x, ya, yb, yc, yd, w_bf16, *, tm=512):
    m, d = x.shape
    c = ya.shape[1]
    tm = _row_tile(m, tm)
    return pl.pallas_call(
        _out_proj_kernel,
        grid=(m // tm,),
        in_specs=[pl.BlockSpec((tm, d), lambda i: (i, 0))] + [pl.BlockSpec((tm, c), lambda i: (i, 0))] * 4
                 + [pl.BlockSpec((d, d), lambda i: (0, 0))],
        out_specs=pl.BlockSpec((tm, d), lambda i: (i, 0)),
        out_shape=jax.ShapeDtypeStruct((m, d), F32),
        compiler_params=_params("parallel"),
        name="out_proj",
    )(x, ya, yb, yc, yd, w_bf16)


def _norm_matmul_kernel(x_ref, g_ref, w_ref, o_ref):
    o_ref[...] = _dot(_rms(x_ref[...], g_ref[...]).astype(BF16), w_ref[...])


def norm_matmul(x, g, w_bf16, *, tm=512):
    m, d = x.shape
    n = w_bf16.shape[1]
    tm = _row_tile(m, tm)
    return pl.pallas_call(
        _norm_matmul_kernel,
        grid=(m // tm,),
        in_specs=[pl.BlockSpec((tm, d), lambda i: (i, 0)),
                  pl.BlockSpec((1, d), lambda i: (0, 0)),
                  pl.BlockSpec((d, n), lambda i: (0, 0))],
        out_specs=pl.BlockSpec((tm, n), lambda i: (i, 0)),
        out_shape=jax.ShapeDtypeStruct((m, n), F32),
        compiler_params=_params("parallel"),
        name="norm_matmul",
    )(x, g.reshape(1, d), w_bf16)


def _mem_attend(q, mk_ref, mv_ref):
    dh = q.shape[-1] // MEM_HEADS
    head = lambda h: slice(h * dh, (h + 1) * dh)
    scores = [_dot_nt(q[:, head(h)], mk_ref[0, :, head(h)].astype(BF16)) for h in range(MEM_HEADS)]
    probs = []
    for s in scores:
        e = jnp.exp(s - jnp.max(s, axis=-1, keepdims=True))
        probs.append((e / jnp.sum(e, axis=-1, keepdims=True)).astype(BF16))
    return [_dot(p, mv_ref[0, :, head(h)].astype(BF16)) for h, p in enumerate(probs)]


def _cross_kernel(x_ref, g_ref, wq_ref, mk_ref, mv_ref, wo_ref, o_ref, *, scale):
    x = x_ref[0]
    q = (_dot(_rms(x, g_ref[...]).astype(BF16), wq_ref[...]) * scale).astype(BF16)
    outs = [o.astype(BF16) for o in _mem_attend(q, mk_ref, mv_ref)]
    o_ref[0] = x + _dot(jnp.concatenate(outs, axis=1), wo_ref[...])


def cross_block(x, g, wq_bf16, mk, mv, wo_bf16, *, tm=512):
    b, t, d = x.shape
    n_mem = mk.shape[1]
    tm = _row_tile(t, tm)
    scale = float((d // MEM_HEADS) ** -0.5)
    const = lambda a: pl.BlockSpec(a.shape, lambda bi, i: (0,) * a.ndim)
    g2 = g.reshape(1, d)
    mem_spec = pl.BlockSpec((1, n_mem, d), lambda bi, i: (bi, 0, 0))
    return pl.pallas_call(
        functools.partial(_cross_kernel, scale=scale),
        grid=(b, t // tm),
        in_specs=[pl.BlockSpec((1, tm, d), lambda bi, i: (bi, i, 0)), const(g2), const(wq_bf16),
                  mem_spec, mem_spec, const(wo_bf16)],
        out_specs=pl.BlockSpec((1, tm, d), lambda bi, i: (bi, i, 0)),
        out_shape=jax.ShapeDtypeStruct((b, t, d), F32),
        compiler_params=_params("parallel", "parallel"),
        name="cross_block",
    )(x, g2, wq_bf16, mk, mv, wo_bf16)


def _mem_attend_kernel(q_ref, mk_ref, mv_ref, o_ref, *, scale):
    q = (q_ref[0] * scale).astype(BF16)
    o_ref[0] = jnp.concatenate(_mem_attend(q, mk_ref, mv_ref), axis=1)


def mem_attend(q, mk, mv, layer):
    b, rows, d = q.shape
    n_mem = mk.shape[2]
    spec = lambda r: pl.BlockSpec((1, r, d), lambda bi: (bi, 0, 0))
    mem_spec = pl.BlockSpec((None, 1, n_mem, d), lambda bi: (layer, bi, 0, 0))
    return pl.pallas_call(
        functools.partial(_mem_attend_kernel, scale=float((d // MEM_HEADS) ** -0.5)),
        grid=(b,),
        in_specs=[spec(rows), mem_spec, mem_spec],
        out_specs=spec(rows),
        out_shape=jax.ShapeDtypeStruct((b, rows, d), F32),
        compiler_params=_params("parallel"),
        name="mem_attend",
    )(q, mk, mv)


def _matmul_residual_kernel(x_ref, y_ref, w_ref, o_ref):
    o_ref[...] = x_ref[...] + _dot(y_ref[...].astype(BF16), w_ref[...])


def matmul_residual(x, y, w_bf16, *, tm=512):
    m, d = x.shape
    tm = _row_tile(m, tm)
    row = pl.BlockSpec((tm, d), lambda i: (i, 0))
    return pl.pallas_call(
        _matmul_residual_kernel,
        grid=(m // tm,),
        in_specs=[row, row, pl.BlockSpec((d, d), lambda i: (0, 0))],
        out_specs=row,
        out_shape=jax.ShapeDtypeStruct((m, d), F32),
        compiler_params=_params("parallel"),
        name="matmul_residual",
    )(x, y, w_bf16)


def _swiglu_chunk(h_ref, wg_ref, wu_ref, wd_ref, acc_ref):
    h = h_ref[...]
    tf = wg_ref.shape[-1]
    piece = MXU_WIDTH if tf % MXU_WIDTH == 0 else tf
    cols = [slice(p, p + piece) for p in range(0, tf, piece)]
    gates = [_dot(h, wg_ref[0, :, c]) for c in cols]
    ups = [_dot(h, wu_ref[0, :, c]) for c in cols]
    acts = [(g * jax.nn.sigmoid(g) * u).astype(BF16) for g, u in zip(gates, ups)]
    downs = [_dot(a, wd_ref[0, c, :]) for a, c in zip(acts, cols)]
    acc_ref[...] += functools.reduce(lambda x, y: x + y, downs)


def _ffn_kernel(x_ref, g_ref, wg_ref, wu_ref, wd_ref, o_ref, h_ref, acc_ref):
    j = pl.program_id(1)

    @pl.when(j == 0)
    def _():
        h_ref[...] = _rms(x_ref[...], g_ref[...]).astype(BF16)
        acc_ref[...] = jnp.zeros_like(acc_ref)

    _swiglu_chunk(h_ref, wg_ref, wu_ref, wd_ref, acc_ref)

    @pl.when(j == pl.num_programs(1) - 1)
    def _():
        o_ref[...] = x_ref[...] + acc_ref[...]


def dense_ffn(x, g, wg, wu, wd, *, tm=1024, tf=512):
    m, d = x.shape
    f = wg.shape[1]
    tm, tf = _row_tile(m, tm), _row_tile(f, tf)
    return pl.pallas_call(
        _ffn_kernel,
        grid=(m // tm, f // tf),
        in_specs=[pl.BlockSpec((tm, d), lambda i, j: (i, 0)),
                  pl.BlockSpec((1, d), lambda i, j: (0, 0)),
                  pl.BlockSpec((1, d, tf), lambda i, j: (0, 0, j)),
                  pl.BlockSpec((1, d, tf), lambda i, j: (0, 0, j)),
                  pl.BlockSpec((1, tf, d), lambda i, j: (0, j, 0))],
        out_specs=pl.BlockSpec((tm, d), lambda i, j: (i, 0)),
        scratch_shapes=[pltpu.VMEM((tm, d), BF16), pltpu.VMEM((tm, d), F32)],
        out_shape=jax.ShapeDtypeStruct((m, d), F32),
        compiler_params=_params("parallel", "arbitrary"),
        name="dense_ffn",
    )(x, g.reshape(1, d), wg[None], wu[None], wd[None])


def _moe_ffn_kernel(te_ref, first_ref, x_ref, wg_ref, wu_ref, wd_ref, o_ref,
                    wg_res_ref, wu_res_ref, wd_res_ref, h_ref, acc_ref):
    i, j = pl.program_id(0), pl.program_id(1)

    @pl.when(first_ref[i] > 0)
    def _():
        wg_res_ref[j] = wg_ref[0].astype(BF16)
        wu_res_ref[j] = wu_ref[0].astype(BF16)
        wd_res_ref[j] = wd_ref[0].astype(BF16)

    @pl.when(j == 0)
    def _():
        h_ref[...] = x_ref[...].astype(BF16)
        acc_ref[...] = jnp.zeros_like(acc_ref)

    _swiglu_chunk(h_ref, wg_res_ref.at[pl.ds(j, 1)], wu_res_ref.at[pl.ds(j, 1)], wd_res_ref.at[pl.ds(j, 1)], acc_ref)

    @pl.when(j == pl.num_programs(1) - 1)
    def _():
        o_ref[...] = acc_ref[...]


def moe_ffn(x, wg, wu, wd, tile_expert, *, tm, tf=512):
    n_rows, d = x.shape
    f = wg.shape[2]
    tf = _row_tile(f, tf)
    assert n_rows % tm == 0 and tf % LANES == 0
    chunks = f // tf
    n_tiles = n_rows // tm
    changed = tile_expert[1:] != tile_expert[:-1]
    tile_first = jnp.concatenate([jnp.ones((1,), jnp.int32), changed.astype(jnp.int32)])
    chunk = lambda i, j, first: jnp.where(first[i] > 0, j, chunks - 1)
    return pl.pallas_call(
        _moe_ffn_kernel,
        grid_spec=pltpu.PrefetchScalarGridSpec(
            num_scalar_prefetch=2,
            grid=(n_tiles, chunks),
            in_specs=[pl.BlockSpec((tm, d), lambda i, j, te, first: (i, 0)),
                      pl.BlockSpec((1, d, tf), lambda i, j, te, first: (te[i], 0, chunk(i, j, first))),
                      pl.BlockSpec((1, d, tf), lambda i, j, te, first: (te[i], 0, chunk(i, j, first))),
                      pl.BlockSpec((1, tf, d), lambda i, j, te, first: (te[i], chunk(i, j, first), 0))],
            out_specs=pl.BlockSpec((tm, d), lambda i, j, te, first: (i, 0)),
            scratch_shapes=[pltpu.VMEM((chunks, d, tf), BF16), pltpu.VMEM((chunks, d, tf), BF16),
                            pltpu.VMEM((chunks, tf, d), BF16),
                            pltpu.VMEM((tm, d), BF16), pltpu.VMEM((tm, d), F32)]),
        out_shape=jax.ShapeDtypeStruct((n_rows, d), F32),
        compiler_params=_params("arbitrary", "arbitrary"),
        name="moe_ffn",
    )(tile_expert, tile_first, x, wg, wu, wd)


def _router_kernel(x_ref, g_ref, wr_ref, h_ref, gates_ref, sel_ref, *, n_e):
    h = _rms(x_ref[...], g_ref[...])
    h_ref[...] = h
    logits = jnp.dot(h, wr_ref[...], preferred_element_type=F32, precision=lax.Precision.HIGHEST)
    idx = lax.broadcasted_iota(jnp.int32, logits.shape, 1).astype(F32)
    logits = jnp.where(idx < n_e, logits, -jnp.inf)
    m1 = jnp.max(logits, axis=-1, keepdims=True)
    i1 = jnp.min(jnp.where(logits == m1, idx, float(n_e)), axis=-1, keepdims=True)
    rest = jnp.where(idx == i1, -jnp.inf, logits)
    m2 = jnp.max(rest, axis=-1, keepdims=True)
    i2 = jnp.min(jnp.where(rest == m2, idx, float(n_e)), axis=-1, keepdims=True)
    e2 = jnp.exp(m2 - m1)
    g1 = 1.0 / (1.0 + e2)
    g2 = e2 / (1.0 + e2)
    gates_ref[...] = jnp.where(idx == i1, g1, jnp.where(idx == i2, g2, 0.0))
    sel_ref[...] = jnp.where(idx == i1, 1, jnp.where(idx == i2, 1, 0)).astype(jnp.int32)


def router(x, g, w_router, *, tm=512):
    m, d = x.shape
    n_e = w_router.shape[1]
    assert TOP_K <= n_e <= LANES
    tm = _row_tile(m, tm)
    wr = jnp.pad(w_router, ((0, 0), (0, LANES - n_e)))
    h, gates, sel = pl.pallas_call(
        functools.partial(_router_kernel, n_e=n_e),
        grid=(m // tm,),
        in_specs=[pl.BlockSpec((tm, d), lambda i: (i, 0)),
                  pl.BlockSpec((1, d), lambda i: (0, 0)),
                  pl.BlockSpec((d, LANES), lambda i: (0, 0))],
        out_specs=[pl.BlockSpec((tm, d), lambda i: (i, 0)),
                   pl.BlockSpec((tm, LANES), lambda i: (i, 0)),
                   pl.BlockSpec((tm, LANES), lambda i: (i, 0))],
        out_shape=[jax.ShapeDtypeStruct((m, d), F32),
                   jax.ShapeDtypeStruct((m, LANES), F32),
                   jax.ShapeDtypeStruct((m, LANES), jnp.int32)],
        compiler_params=_params("parallel"),
        name="router",
    )(x, g.reshape(1, d), wr)
    return h, gates[:, :n_e], sel[:, :n_e]


def _combine_kernel(x_ref, y1_ref, y2_ref, gt_ref, g_ref, o_ref, *, normed):
    gt = gt_ref[...]
    y = x_ref[...] + gt[:, 0:1] * y1_ref[...] + gt[:, 1:2] * y2_ref[...]
    o_ref[...] = _rms(y, g_ref[...]) if normed else y


def moe_combine(x, y1, y2, gates2, g_out, *, tm=512):
    m, d = x.shape
    tm = _row_tile(m, tm)
    spec = pl.BlockSpec((tm, d), lambda i: (i, 0))
    normed = g_out is not None
    g_row = (g_out if normed else jnp.ones((d,), F32)).reshape(1, d)
    return pl.pallas_call(
        functools.partial(_combine_kernel, normed=normed),
        grid=(m // tm,),
        in_specs=[spec, spec, spec, pl.BlockSpec((tm, TOP_K), lambda i: (i, 0)),
                  pl.BlockSpec((1, d), lambda i: (0, 0))],
        out_specs=spec,
        out_shape=jax.ShapeDtypeStruct((m, d), F32),
        compiler_params=_params("parallel"),
        name="moe_combine",
    )(x, y1, y2, gates2, g_row)


MOE_TILE = 512


def moe_layer(xs, g, w_router, wg, wu, wd, g_out=None):
    n_e = w_router.shape[1]
    tm = MOE_TILE
    routed = [router(x, g, w_router) for x in xs]
    h, gates8, sel8 = (jnp.concatenate(parts, axis=0) for parts in zip(*routed))
    m = h.shape[0]
    lanes = jnp.arange(n_e, dtype=jnp.int32)
    ids = jnp.stack([jnp.min(jnp.where(sel8 > 0, lanes, n_e), axis=-1),
                     jnp.max(jnp.where(sel8 > 0, lanes, -1), axis=-1)], axis=-1)
    pick = lambda table: jnp.stack([jnp.sum(jnp.where(lanes == ids[:, k:k + 1], table, 0), axis=-1)
                                    for k in range(TOP_K)], axis=-1)
    gates2 = pick(gates8)
    n_tiles = (m * TOP_K + n_e * (tm - 1)) // tm
    rank = jnp.cumsum(sel8, axis=0) - sel8
    counts = rank[-1] + sel8[-1]
    padded = ((counts + tm - 1) // tm) * tm
    pad_end = jnp.cumsum(padded)
    pad_start = pad_end - padded
    pos2 = pick(pad_start[None, :] + rank)
    token = jnp.broadcast_to(jnp.arange(m, dtype=jnp.int32)[:, None], (m, TOP_K))
    src_token = jnp.zeros((n_tiles * tm,), jnp.int32).at[pos2.reshape(-1)].set(token.reshape(-1))
    tile_start = jnp.arange(n_tiles, dtype=jnp.int32) * tm
    tile_expert = jnp.minimum(jnp.sum(tile_start[:, None] >= pad_end[None, :], axis=-1), n_e - 1).astype(jnp.int32)
    h_sorted = h.at[src_token].get(mode="promise_in_bounds")
    y_sorted = moe_ffn(h_sorted, wg, wu, wd, tile_expert, tm=tm)
    results, row0 = [], 0
    for x in xs:
        rows = slice(row0, row0 + x.shape[0])
        y1 = y_sorted.at[pos2[rows, 0]].get(mode="promise_in_bounds")
        y2 = y_sorted.at[pos2[rows, 1]].get(mode="promise_in_bounds")
        results.append(moe_combine(x, y1, y2, gates2[rows], g_out))
        row0 += x.shape[0]
    return results


def _final_norm_kernel(x_ref, g_ref, o_ref):
    o_ref[...] = _rms(x_ref[...], g_ref[...])


def final_norm(x, g, *, tm=512):
    m, d = x.shape
    tm = _row_tile(m, tm)
    return pl.pallas_call(
        _final_norm_kernel,
        grid=(m // tm,),
        in_specs=[pl.BlockSpec((tm, d), lambda i: (i, 0)), pl.BlockSpec((1, d), lambda i: (0, 0))],
        out_specs=pl.BlockSpec((tm, d), lambda i: (i, 0)),
        out_shape=jax.ShapeDtypeStruct((m, d), F32),
        compiler_params=_params("parallel"),
        name="final_norm",
    )(x, g.reshape(1, d))


def _block_diag(w):
    g, c, _ = w.shape
    eye = jnp.eye(g, dtype=w.dtype)
    return (eye[:, None, :, None] * w[:, :, None, :]).reshape(g * c, g * c)


def _pad_time(a, t):
    return jnp.pad(a, ((0, 0), (0, t - a.shape[1]), (0, 0)))


def kernel(x_prompt, x_sample, cache_k, cache_v, cache_mem_k, cache_mem_v, state_pool, state_sconv, state_conf, page_table, mem_prompt, g_mix, w_in, w_pool, pool_scale, w_sconv, sb_bias, w_conf, b_conf, g_conf_ln, b_conf_ln, w_out, g_cross, g_mem, w_mq, w_mk, w_mv, w_mo, g_ffn, w_ff_gate, w_ff_up, w_ff_down, w_router, w_moe_gate, w_moe_up, w_moe_down, g_final):
    bp, tp, d = x_prompt.shape
    bs, ts, _ = x_sample.shape
    depth = g_mix.shape[0]
    c = d // 4
    dh = c // SB_HEADS
    page = cache_k.shape[2]
    past = page_table.shape[1] * page
    n_mem = mem_prompt.shape[1]
    qscale = float(dh ** -0.5)
    assert ts <= SAMPLE_ROWS and ts <= page

    bf = lambda a: a.astype(BF16)
    w_in_b, w_out_b, w_mq_b, w_mk_b, w_mv_b, w_mo_b = map(bf, (w_in, w_out, w_mq, w_mk, w_mv, w_mo))
    cache_kt = cache_k.transpose(0, 1, 3, 4, 2).reshape(depth, -1, c, page)
    cache_vt = cache_v.transpose(0, 1, 3, 4, 2).reshape(depth, -1, c, page)
    head_mask = (jnp.arange(c)[None, :] // dh == jnp.arange(SB_HEADS)[:, None]).astype(F32)
    mem_k_all = cache_mem_k.reshape(depth, bs, cache_mem_k.shape[2], d)
    mem_v_all = cache_mem_v.reshape(depth, bs, cache_mem_v.shape[2], d)

    xp = x_prompt.reshape(bp * tp, d)
    xs = x_sample.reshape(bs * ts, d)
    zeros_state = lambda keep: jnp.zeros((bp, keep, c), F32)
    outs = {name: [] for name in ("kp", "vp", "ks", "vs", "mkp", "mvp", "poolp", "pools", "scp", "scs", "cfp", "cfs")}

    for i in range(depth):
        wpool_bd = bf(_block_diag(w_pool[i]))
        mixer_w = (wpool_bd, pool_scale[i], w_sconv[i], w_conf[i], b_conf[i], g_conf_ln[i], b_conf_ln[i])

        pa, pb, q, k_t, v_t, kb, vb, pd = in_proj(xp, g_mix[i], w_in_b[i], qscale=qscale, seq_len=tp)
        r3 = lambda a: a.reshape(bp, tp, a.shape[-1])
        ya, yb, yd, n_pool, n_sc, n_cf = local_mixers(
            r3(pa), r3(pb), r3(pd), zeros_state(max(POOL_WINDOWS) - 1), zeros_state(w_sconv.shape[1] - 1),
            zeros_state(w_conf.shape[1] - 1), *mixer_w, t_valid=tp, pos0=0)
        yc = sb_prompt(r3(q), r3(kb), r3(vb), sb_bias[i])
        f2 = lambda a: a.reshape(bp * tp, c)
        xp = out_proj(xp, f2(ya), f2(yb), f2(yc), f2(yd), w_out_b[i])
        heads_last = lambda a_t: a_t.reshape(bp, SB_HEADS, dh, tp).transpose(0, 3, 1, 2)
        outs["kp"].append(heads_last(k_t))
        outs["vp"].append(heads_last(v_t))
        outs["poolp"].append(n_pool); outs["scp"].append(n_sc); outs["cfp"].append(n_cf)

        mem2 = mem_prompt.reshape(bp * n_mem, d)
        mk_p = norm_matmul(mem2, g_mem[i], w_mk_b[i]).reshape(bp, n_mem, d)
        mv_p = norm_matmul(mem2, g_mem[i], w_mv_b[i]).reshape(bp, n_mem, d)
        outs["mkp"].append(mk_p.reshape(bp, n_mem, MEM_HEADS, d // MEM_HEADS))
        outs["mvp"].append(mv_p.reshape(bp, n_mem, MEM_HEADS, d // MEM_HEADS))
        xp = cross_block(xp.reshape(bp, tp, d), g_cross[i], w_mq_b[i], mk_p, mv_p, w_mo_b[i]).reshape(bp * tp, d)

        pa, pb, q, k, v, kb, vb, pd = in_proj(xs, g_mix[i], w_in_b[i], qscale=qscale)
        r3 = lambda a: _pad_time(a.reshape(bs, ts, a.shape[-1]), SAMPLE_ROWS)
        ya, yb, yd, n_pool, n_sc, n_cf = local_mixers(
            r3(pa), r3(pb), r3(pd), state_pool[i], state_sconv[i], state_conf[i], *mixer_w,
            t_valid=ts, pos0=past)
        q_s = _pad_time(q.astype(F32).reshape(bs, ts, c), SAMPLE_ROWS)
        q_bd = (q_s[:, None] * head_mask[None, :, None]).reshape(bs, SB_HEADS * SAMPLE_ROWS, c)
        new_t = lambda a: jnp.pad(a.reshape(bs, ts, c).transpose(0, 2, 1), ((0, 0), (0, 0), (0, page - ts)))
        yc = sb_sample(q_bd, new_t(k), new_t(v), cache_kt, cache_vt, i, page_table, sb_bias[i])
        yc = yc[:, :ts].reshape(bs * ts, c)
        f2 = lambda a: a[:, :ts].reshape(bs * ts, c)
        xs = out_proj(xs, f2(ya), f2(yb), yc, f2(yd), w_out_b[i])
        outs["ks"].append(k.reshape(bs, ts, SB_HEADS, dh))
        outs["vs"].append(v.reshape(bs, ts, SB_HEADS, dh))
        outs["pools"].append(n_pool); outs["scs"].append(n_sc); outs["cfs"].append(n_cf)

        q_mem = _pad_time(norm_matmul(xs, g_cross[i], w_mq_b[i]).reshape(bs, ts, d), 2 * SUBLANES)
        o_mem = mem_attend(q_mem, mem_k_all, mem_v_all, i)
        xs = matmul_residual(xs, o_mem[:, :ts].reshape(bs * ts, d), w_mo_b[i])

        j = i // 2
        if i % 2 == 0:
            wg, wu, wd = bf(w_ff_gate[j]), bf(w_ff_up[j]), bf(w_ff_down[j])
            xp = dense_ffn(xp, g_ffn[i], wg, wu, wd)
            xs = dense_ffn(xs, g_ffn[i], wg, wu, wd)
        else:
            xp, xs = moe_layer([xp, xs], g_ffn[i], w_router[j], w_moe_gate[j], w_moe_up[j], w_moe_down[j],
                               g_out=g_final if i == depth - 1 else None)

    if depth % 2 == 1:
        xp, xs = final_norm(xp, g_final), final_norm(xs, g_final)
    y_prompt = xp.reshape(bp, tp, d)
    y_sample = xs.reshape(bs, ts, d)
    st = lambda name: jnp.stack(outs[name])
    return (y_prompt, y_sample, st("kp"), st("vp"), st("ks"), st("vs"), st("mkp"), st("mvp"),
            st("poolp"), st("pools"), st("scp"), st("scs"), st("cfp"), st("cfs"))
```

```python
import functools

import jax
import jax.numpy as jnp
from jax import lax
from jax.experimental import pallas as pl
from jax.experimental.pallas import tpu as pltpu

EPS = 1e-6
SB_HEADS = 4
MEM_HEADS = 4
POOL_WINDOWS = (2, 4, 8, 16)
TOP_K = 2
LANES = 128
SUBLANES = 8
MXU_WIDTH = 256
VMEM_LIMIT_BYTES = 56 * 1024 * 1024
BF16 = jnp.bfloat16
F32 = jnp.float32


def _params(*sem):
    return pltpu.CompilerParams(dimension_semantics=sem, vmem_limit_bytes=VMEM_LIMIT_BYTES)


def _rms(x, g):
    return x * lax.rsqrt(jnp.mean(x * x, axis=-1, keepdims=True) + EPS) * g


def _dot(a, b):
    return jnp.dot(a, b, preferred_element_type=F32)


def _dot_nt(a, b):
    return lax.dot_general(a, b, (((1,), (1,)), ((), ())), preferred_element_type=F32)


def _row_tile(m, want):
    if m <= want:
        return m
    for t in range(want - want % SUBLANES, 0, -SUBLANES):
        if m % t == 0:
            return t
    raise ValueError(f"no aligned row tile for {m}")


def _in_proj_kernel(x_ref, g_ref, w_ref, pa_ref, pb_ref, q_ref, k_ref, v_ref, kb_ref, vb_ref, pd_ref,
                    *, c, qscale, kv_transposed):
    h = _rms(x_ref[...], g_ref[...]).astype(BF16)
    pa_ref[...] = _dot(h, w_ref[:, 0:c])
    pb_ref[...] = _dot(h, w_ref[:, c:4 * c])
    q_ref[...] = (_dot(h, w_ref[:, 4 * c:5 * c]) * qscale).astype(BF16)
    k = _dot(h, w_ref[:, 5 * c:6 * c])
    v = _dot(h, w_ref[:, 6 * c:7 * c])
    if kv_transposed:
        k_ref[0] = k.T
        v_ref[0] = v.T
    else:
        k_ref[...] = k
        v_ref[...] = v
    kb_ref[...] = k.astype(BF16)
    vb_ref[...] = v.astype(BF16)
    pd_ref[...] = _dot(h, w_ref[:, 7 * c:9 * c])


def in_proj(x, g, w_bf16, *, qscale, seq_len=None, tm=512):
    m, d = x.shape
    c = w_bf16.shape[1] // 9
    tm = _row_tile(m if seq_len is None else seq_len, tm)
    row = lambda wd: pl.BlockSpec((tm, wd), lambda i: (i, 0))
    rows = lambda wd, dt: jax.ShapeDtypeStruct((m, wd), dt)
    if seq_len is None:
        kv_spec, kv_shape = row(c), rows(c, F32)
    else:
        tiles = seq_len // tm
        kv_spec = pl.BlockSpec((1, c, tm), lambda i: (i // tiles, 0, i % tiles))
        kv_shape = jax.ShapeDtypeStruct((m // seq_len, c, seq_len), F32)
    return pl.pallas_call(
        functools.partial(_in_proj_kernel, c=c, qscale=qscale, kv_transposed=seq_len is not None),
        grid=(m // tm,),
        in_specs=[pl.BlockSpec((tm, d), lambda i: (i, 0)),
                  pl.BlockSpec((1, d), lambda i: (0, 0)),
                  pl.BlockSpec((d, 9 * c), lambda i: (0, 0))],
        out_specs=[row(c), row(3 * c), row(c), kv_spec, kv_spec, row(c), row(c), row(2 * c)],
        out_shape=[rows(c, F32), rows(3 * c, F32), rows(c, BF16), kv_shape, kv_shape, rows(c, BF16), rows(c, BF16),
                   rows(2 * c, F32)],
        compiler_params=_params("parallel"),
        name="in_proj",
    )(x, g.reshape(1, d), w_bf16)


POOL_HALO = 16
SCONV_HALO = 8
CONF_HALO = 32


def _stage(buf_ref, state_ref, new_rows, halo, tt, first):
    @pl.when(first)
    def _():
        buf_ref[0:halo, :] = state_ref[0]
    buf_ref[halo:halo + tt, :] = new_rows


def _carry_history(buf_ref, halo, tt):
    buf_ref[0:halo, :] = buf_ref[tt:tt + halo, :]


def _causal_conv(buf_ref, w_ref, halo, tt, width):
    full = buf_ref[...]
    rolled = {0: full}
    acc = None
    for j in range(width):
        off = halo - (width - 1) + j
        shift = (-off) % SUBLANES
        if shift not in rolled:
            rolled[shift] = pltpu.roll(full, shift, axis=0)
        start = off + shift
        term = rolled[shift][start:start + tt, :] * w_ref[j:j + 1, :]
        acc = term if acc is None else acc + term
    return acc


def _mixers_kernel(pa_ref, pb_ref, pd_ref, spool_ref, ssc_ref, scf_ref,
                   wpool_ref, pscale_ref, wsc_ref, wcf_ref, bcf_ref, gln_ref, bln_ref,
                   ya_ref, yb_ref, yd_ref, npool_ref, nsc_ref, ncf_ref,
                   bufa_ref, bufb_ref, bufd_ref, *, tt, n_tiles, t_valid, pos0, sc_width, cf_width):
    i = pl.program_id(1)
    first = i == 0
    c = ya_ref.shape[-1]

    u = pa_ref[0]
    _stage(bufa_ref, spool_ref, u, POOL_HALO, tt, first)
    sums, acc, width = {}, bufa_ref[...], 1
    for w in POOL_WINDOWS:
        assert w == 2 * width and w <= POOL_HALO
        acc = acc + pltpu.roll(acc, width, axis=0)
        sums[w], width = acc[POOL_HALO:POOL_HALO + tt, :], w
    lane = lax.broadcasted_iota(jnp.int32, (tt, c), 1)
    grp = lane // (c // len(POOL_WINDOWS))
    win = sums[POOL_WINDOWS[-1]]
    wlen = jnp.full((tt, c), POOL_WINDOWS[-1], jnp.int32)
    for gi in range(len(POOL_WINDOWS) - 2, -1, -1):
        win = jnp.where(grp == gi, sums[POOL_WINDOWS[gi]], win)
        wlen = jnp.where(grp == gi, POOL_WINDOWS[gi], wlen)
    pos = pos0 + i * tt + lax.broadcasted_iota(jnp.int32, (tt, c), 0)
    cnt = jnp.minimum(pos + 1, wlen).astype(F32)
    p = win / cnt - u
    ya_ref[0] = _dot(p.astype(BF16), wpool_ref[...]) * pscale_ref[...]

    pb = pb_ref[0]
    hb, cb, bb = pb[:, 0:c], pb[:, c:2 * c], pb[:, 2 * c:3 * c]
    _stage(bufb_ref, ssc_ref, cb * hb, SCONV_HALO, tt, first)
    yb_ref[0] = bb * _causal_conv(bufb_ref, wsc_ref, SCONV_HALO, tt, sc_width)

    pd = pd_ref[0]
    ad, gd = pd[:, 0:c], pd[:, c:2 * c]
    _stage(bufd_ref, scf_ref, ad * jax.nn.sigmoid(gd), CONF_HALO, tt, first)
    cv = _causal_conv(bufd_ref, wcf_ref, CONF_HALO, tt, cf_width) + bcf_ref[...]
    xc = cv - jnp.mean(cv, axis=-1, keepdims=True)
    ln = xc * lax.rsqrt(jnp.mean(xc * xc, axis=-1, keepdims=True) + EPS) * gln_ref[...] + bln_ref[...]
    yd_ref[0] = ln * jax.nn.sigmoid(ln)

    @pl.when(i == n_tiles - 1)
    def _():
        for out_ref, buf_ref, halo, keep in ((npool_ref, bufa_ref, POOL_HALO, max(POOL_WINDOWS) - 1),
                                             (nsc_ref, bufb_ref, SCONV_HALO, sc_width - 1),
                                             (ncf_ref, bufd_ref, CONF_HALO, cf_width - 1)):
            end = halo + t_valid
            out_ref[0] = buf_ref[end - keep:end, :]

    if n_tiles > 1:
        _carry_history(bufa_ref, POOL_HALO, tt)
        _carry_history(bufb_ref, SCONV_HALO, tt)
        _carry_history(bufd_ref, CONF_HALO, tt)


def _pad_state(state, halo):
    return jnp.pad(state, ((0, 0), (halo - state.shape[1], 0), (0, 0)))


def local_mixers(pa, pb, pd, st_pool, st_sc, st_cf, wpool_bd, pool_scale, w_sc, w_cf, b_cf, g_ln, b_ln,
                 *, t_valid, pos0, tt=512):
    b, t, c = pa.shape
    tt = _row_tile(t, tt)
    n_tiles = t // tt
    assert n_tiles == 1 or tt >= CONF_HALO
    sc_width, cf_width = w_sc.shape[0], w_cf.shape[0]
    keep_pool = max(POOL_WINDOWS) - 1
    tile = lambda wd: pl.BlockSpec((1, tt, wd), lambda bi, i: (bi, i, 0))
    per_b = lambda r, wd: pl.BlockSpec((1, r, wd), lambda bi, i: (bi, 0, 0))
    full2 = lambda a: pl.BlockSpec(a.shape, lambda bi, i: (0, 0))
    row = lambda a: a.reshape(1, c)
    weights = (wpool_bd, row(pool_scale), w_sc, w_cf, row(b_cf), row(g_ln), row(b_ln))
    last_valid = t_valid - (n_tiles - 1) * tt
    return pl.pallas_call(
        functools.partial(_mixers_kernel, tt=tt, n_tiles=n_tiles, t_valid=last_valid, pos0=pos0,
                          sc_width=sc_width, cf_width=cf_width),
        grid=(b, n_tiles),
        in_specs=[tile(c), tile(3 * c), tile(2 * c),
                  per_b(POOL_HALO, c), per_b(SCONV_HALO, c), per_b(CONF_HALO, c)] + [full2(a) for a in weights],
        out_specs=[tile(c), tile(c), tile(c),
                   per_b(keep_pool, c), per_b(sc_width - 1, c), per_b(cf_width - 1, c)],
        out_shape=[jax.ShapeDtypeStruct((b, t, c), F32)] * 3 + [
            jax.ShapeDtypeStruct((b, keep_pool, c), F32),
            jax.ShapeDtypeStruct((b, sc_width - 1, c), F32),
            jax.ShapeDtypeStruct((b, cf_width - 1, c), F32)],
        scratch_shapes=[pltpu.VMEM((POOL_HALO + tt, c), F32),
                        pltpu.VMEM((SCONV_HALO + tt, c), F32),
                        pltpu.VMEM((CONF_HALO + tt, c), F32)],
        compiler_params=_params("parallel", "arbitrary"),
        name="local_mixers",
    )(pa, pb, pd, _pad_state(st_pool, POOL_HALO), _pad_state(st_sc, SCONV_HALO), _pad_state(st_cf, CONF_HALO),
      *weights)


def _suffix_matrix(tk):
    r = lax.broadcasted_iota(jnp.int32, (tk, tk), 0)
    s = lax.broadcasted_iota(jnp.int32, (tk, tk), 1)
    return jnp.where(r > s, 1.0, 0.0).astype(BF16)


LOG2E = 1.4426950408889634
_SIGN_BIT = 0x80000000


def _softplus2(z2, mask):
    neg_abs = lax.bitcast_convert_type(lax.bitcast_convert_type(z2, jnp.uint32) | jnp.uint32(_SIGN_BIT), F32)
    sp = jnp.maximum(z2, 0.0) + jnp.log2(1.0 + jnp.exp2(neg_abs))
    return sp if mask is None else jnp.where(mask, sp, 0.0)


def _later_sum(sp, suffix):
    return _dot(sp.astype(BF16), suffix)


def _sb_weight(z2, sp, later, carry, mask):
    a = jnp.exp2(z2 - sp - later - carry)
    return (a if mask is None else jnp.where(mask, a, 0.0)).astype(BF16)


def _head_select(x, c, heads):
    lane_head = lax.broadcasted_iota(jnp.int32, x.shape, x.ndim - 1) // (c // heads)
    return [jnp.where(lane_head == h, x, jnp.zeros_like(x)) for h in range(heads)]


SB_BLOCKS_PER_ITERATION = 4


def _sb_prompt_kernel(bias_ref, q_ref, k_ref, v_ref, o_ref, acc_ref, *, tq):
    qi = pl.program_id(1)
    c = q_ref.shape[-1]
    qh = _head_select(q_ref[0], c, SB_HEADS)
    suffix = _suffix_matrix(tq)
    row = lax.broadcasted_iota(jnp.int32, (tq, tq), 0)
    col = lax.broadcasted_iota(jnp.int32, (tq, tq), 1)
    diag_mask = col < row

    bias2 = [bias_ref[h] * LOG2E for h in range(SB_HEADS)]
    heads = range(SB_HEADS)

    def key_blocks(kbis, carries, mask, first):
        starts = [pl.multiple_of(kbi * tq, tq) for kbi in kbis]
        z2 = [[_dot_nt(qh[h], k_ref[0, pl.ds(s, tq), :]) * LOG2E + bias2[h] for h in heads] for s in starts]
        sp = [[_softplus2(z, mask) for z in zb] for zb in z2]
        later = [[_later_sum(x, suffix) for x in sb] for sb in sp]
        carries = list(carries)
        outs = [None] * SB_HEADS
        for bi, s in enumerate(starts):
            vb = v_ref[0, pl.ds(s, tq), :]
            for h in heads:
                o = _dot(_sb_weight(z2[bi][h], sp[bi][h], later[bi][h], carries[h], mask), vb)
                outs[h] = o if outs[h] is None else outs[h] + o
                carries[h] = carries[h] + jnp.sum(sp[bi][h], axis=1, keepdims=True)
        for h in heads:
            if first:
                acc_ref[h] = outs[h]
            else:
                acc_ref[h] += outs[h]
        return tuple(carries)

    zero = jnp.zeros((tq, 1), F32)
    carries = key_blocks([qi], (zero,) * SB_HEADS, diag_mask, True)
    group = SB_BLOCKS_PER_ITERATION
    assert group & (group - 1) == 0
    n_groups = lax.shift_right_logical(qi, group.bit_length() - 1)
    carries = lax.fori_loop(
        0, n_groups, lambda j, cs: key_blocks([qi - 1 - group * j - u for u in range(group)], cs, None, False),
        carries)
    rest = lax.bitwise_and(qi, group - 1)
    lax.fori_loop(0, rest, lambda j, cs: key_blocks([rest - 1 - j], cs, None, False), carries)

    parts = [_head_select(acc_ref[h], c, SB_HEADS)[h] for h in range(SB_HEADS)]
    o_ref[0] = functools.reduce(lambda x, y: x + y, parts)


def sb_prompt(q, k, v, bias, *, tq=256):
    b, t, c = q.shape
    tq = _row_tile(t, tq)
    return pl.pallas_call(
        functools.partial(_sb_prompt_kernel, tq=tq),
        grid_spec=pltpu.PrefetchScalarGridSpec(
            num_scalar_prefetch=1,
            grid=(b, t // tq),
            in_specs=[pl.BlockSpec((1, tq, c), lambda bi, i, *_: (bi, i, 0)),
                      pl.BlockSpec((1, t, c), lambda bi, i, *_: (bi, 0, 0)),
                      pl.BlockSpec((1, t, c), lambda bi, i, *_: (bi, 0, 0))],
            out_specs=pl.BlockSpec((1, tq, c), lambda bi, i, *_: (bi, i, 0)),
            scratch_shapes=[pltpu.VMEM((SB_HEADS, tq, c), F32)]),
        out_shape=jax.ShapeDtypeStruct((b, t, c), F32),
        compiler_params=_params("parallel", "arbitrary"),
        name="sb_prompt",
    )(bias, q, k, v)


SAMPLE_ROWS = 16
PAGES_PER_STEP = 16


def _sb_sample_kernel(pt_ref, bias_ref, q_ref, kn_ref, vn_ref, *rest, n_pages_step):
    k_refs = rest[:n_pages_step]
    v_refs = rest[n_pages_step:2 * n_pages_step]
    o_ref, acc_ref, carry_ref = rest[2 * n_pages_step:]
    j = pl.program_id(1)
    page = kn_ref.shape[-1]
    rows = SB_HEADS * SAMPLE_ROWS
    q_bd = q_ref[0].astype(BF16)
    row = lax.broadcasted_iota(jnp.int32, (rows, 1), 0)
    bias2 = jnp.zeros((rows, 1), F32)
    for h in range(SB_HEADS):
        bias2 = jnp.where(row // SAMPLE_ROWS == h, bias_ref[h] * LOG2E, bias2)
    suffix = _suffix_matrix(page)

    def pages(kt_refs, vt_refs, mask):
        z2 = [_dot(q_bd, kt[...].astype(BF16)) * LOG2E + bias2 for kt in kt_refs]
        sp = [_softplus2(z, mask) for z in z2]
        later = [_later_sum(x, suffix) for x in sp]
        carry = carry_ref[...]
        out = None
        for z, x, lt, vt in zip(z2, sp, later, vt_refs):
            o = _dot_nt(_sb_weight(z, x, lt, carry, mask), vt[...].astype(BF16))
            out = o if out is None else out + o
            carry = carry + jnp.sum(x, axis=1, keepdims=True)
        carry_ref[...] = carry
        return out

    @pl.when(j == 0)
    def _():
        carry_ref[...] = jnp.zeros_like(carry_ref)
        t_new = lax.broadcasted_iota(jnp.int32, (rows, page), 0) % SAMPLE_ROWS
        s_new = lax.broadcasted_iota(jnp.int32, (rows, page), 1)
        acc_ref[...] = pages([kn_ref.at[0]], [vn_ref.at[0]], s_new < t_new)

    acc_ref[...] += pages(k_refs, v_refs, None)

    @pl.when(j == pl.num_programs(1) - 1)
    def _():
        c = acc_ref.shape[-1]
        parts = [_head_select(acc_ref[h * SAMPLE_ROWS:(h + 1) * SAMPLE_ROWS, :], c, SB_HEADS)[h]
                 for h in range(SB_HEADS)]
        o_ref[0] = functools.reduce(lambda x, y: x + y, parts)


def sb_sample(q_bd, kt_new, vt_new, cache_kt, cache_vt, layer, page_table, bias):
    b, rows, c = q_bd.shape
    assert rows == SB_HEADS * SAMPLE_ROWS
    page = cache_kt.shape[-1]
    n_pages = page_table.shape[1]
    pps = PAGES_PER_STEP if n_pages % PAGES_PER_STEP == 0 else 1
    n_steps = n_pages // pps

    def page_spec(p):
        return pl.BlockSpec((None, None, c, page),
                            lambda bi, j, pt, bs: (layer, pt[bi, n_pages - 1 - (j * pps + p)], 0, 0))

    per_b = lambda r, w: pl.BlockSpec((1, r, w), lambda bi, j, *_: (bi, 0, 0))
    return pl.pallas_call(
        functools.partial(_sb_sample_kernel, n_pages_step=pps),
        grid_spec=pltpu.PrefetchScalarGridSpec(
            num_scalar_prefetch=2,
            grid=(b, n_steps),
            in_specs=[per_b(rows, c), per_b(c, page), per_b(c, page)]
                     + [page_spec(p) for p in range(pps)] + [page_spec(p) for p in range(pps)],
            out_specs=per_b(SAMPLE_ROWS, c),
            scratch_shapes=[pltpu.VMEM((rows, c), F32), pltpu.VMEM((rows, 1), F32)]),
        out_shape=jax.ShapeDtypeStruct((b, SAMPLE_ROWS, c), F32),
        compiler_params=_params("parallel", "arbitrary"),
        name="sb_sample",
    )(page_table, bias, q_bd, kt_new, vt_new, *([cache_kt] * pps), *([cache_vt] * pps))


def _out_proj_kernel(x_ref, ya_ref, yb_ref, yc_ref, yd_ref, w_ref, o_ref):
    y = jnp.concatenate([r[...].astype(BF16) for r in (ya_ref, yb_ref, yc_ref, yd_ref)], axis=1)
    o_ref[...] = x_ref[...] + _dot(y, w_ref[...])


def out_proj(x, ya, yb, yc, yd, w_bf16, *, tm=512):
    m, d = x.shape
    c = ya.shape[1]
    tm = _row_tile(m, tm)
    return pl.pallas_call(
        _out_proj_kernel,
        grid=(m // tm,),
        in_specs=[pl.BlockSpec((tm, d), lambda i: (i, 0))] + [pl.BlockSpec((tm, c), lambda i: (i, 0))] * 4
                 + [pl.BlockSpec((d, d), lambda i: (0, 0))],
        out_specs=pl.BlockSpec((tm, d), lambda i: (i, 0)),
        out_shape=jax.ShapeDtypeStruct((m, d), F32),
        compiler_params=_params("parallel"),
        name="out_proj",
    )(x, ya, yb, yc, yd, w_bf16)


def _norm_matmul_kernel(x_ref, g_ref, w_ref, o_ref):
    o_ref[...] = _dot(_rms(x_ref[...], g_ref[...]).astype(BF16), w_ref[...])


def norm_matmul(x, g, w_bf16, *, tm=512):
    m, d = x.shape
    n = w_bf16.shape[1]
    tm = _row_tile(m, tm)
    return pl.pallas_call(
        _norm_matmul_kernel,
        grid=(m // tm,),
        in_specs=[pl.BlockSpec((tm, d), lambda i: (i, 0)),
                  pl.BlockSpec((1, d), lambda i: (0, 0)),
                  pl.BlockSpec((d, n), lambda i: (0, 0))],
        out_specs=pl.BlockSpec((tm, n), lambda i: (i, 0)),
        out_shape=jax.ShapeDtypeStruct((m, n), F32),
        compiler_params=_params("parallel"),
        name="norm_matmul",
    )(x, g.reshape(1, d), w_bf16)


def _mem_attend(q, mk_ref, mv_ref):
    dh = q.shape[-1] // MEM_HEADS
    head = lambda h: slice(h * dh, (h + 1) * dh)
    scores = [_dot_nt(q[:, head(h)], mk_ref[0, :, head(h)].astype(BF16)) for h in range(MEM_HEADS)]
    probs = []
    for s in scores:
        e = jnp.exp(s - jnp.max(s, axis=-1, keepdims=True))
        probs.append((e / jnp.sum(e, axis=-1, keepdims=True)).astype(BF16))
    return [_dot(p, mv_ref[0, :, head(h)].astype(BF16)) for h, p in enumerate(probs)]


def _cross_kernel(x_ref, g_ref, wq_ref, mk_ref, mv_ref, wo_ref, o_ref, *, scale):
    x = x_ref[0]
    q = (_dot(_rms(x, g_ref[...]).astype(BF16), wq_ref[...]) * scale).astype(BF16)
    outs = [o.astype(BF16) for o in _mem_attend(q, mk_ref, mv_ref)]
    o_ref[0] = x + _dot(jnp.concatenate(outs, axis=1), wo_ref[...])


def cross_block(x, g, wq_bf16, mk, mv, wo_bf16, *, tm=512):
    b, t, d = x.shape
    n_mem = mk.shape[1]
    tm = _row_tile(t, tm)
    scale = float((d // MEM_HEADS) ** -0.5)
    const = lambda a: pl.BlockSpec(a.shape, lambda bi, i: (0,) * a.ndim)
    g2 = g.reshape(1, d)
    mem_spec = pl.BlockSpec((1, n_mem, d), lambda bi, i: (bi, 0, 0))
    return pl.pallas_call(
        functools.partial(_cross_kernel, scale=scale),
        grid=(b, t // tm),
        in_specs=[pl.BlockSpec((1, tm, d), lambda bi, i: (bi, i, 0)), const(g2), const(wq_bf16),
                  mem_spec, mem_spec, const(wo_bf16)],
        out_specs=pl.BlockSpec((1, tm, d), lambda bi, i: (bi, i, 0)),
        out_shape=jax.ShapeDtypeStruct((b, t, d), F32),
        compiler_params=_params("parallel", "parallel"),
        name="cross_block",
    )(x, g2, wq_bf16, mk, mv, wo_bf16)


def _mem_attend_kernel(q_ref, mk_ref, mv_ref, o_ref, *, scale):
    q = (q_ref[0] * scale).astype(BF16)
    o_ref[0] = jnp.concatenate(_mem_attend(q, mk_ref, mv_ref), axis=1)


def mem_attend(q, mk, mv, layer):
    b, rows, d = q.shape
    n_mem = mk.shape[2]
    spec = lambda r: pl.BlockSpec((1, r, d), lambda bi: (bi, 0, 0))
    mem_spec = pl.BlockSpec((None, 1, n_mem, d), lambda bi: (layer, bi, 0, 0))
    return pl.pallas_call(
        functools.partial(_mem_attend_kernel, scale=float((d // MEM_HEADS) ** -0.5)),
        grid=(b,),
        in_specs=[spec(rows), mem_spec, mem_spec],
        out_specs=spec(rows),
        out_shape=jax.ShapeDtypeStruct((b, rows, d), F32),
        compiler_params=_params("parallel"),
        name="mem_attend",
    )(q, mk, mv)


def _matmul_residual_kernel(x_ref, y_ref, w_ref, o_ref):
    o_ref[...] = x_ref[...] + _dot(y_ref[...].astype(BF16), w_ref[...])


def matmul_residual(x, y, w_bf16, *, tm=512):
    m, d = x.shape
    tm = _row_tile(m, tm)
    row = pl.BlockSpec((tm, d), lambda i: (i, 0))
    return pl.pallas_call(
        _matmul_residual_kernel,
        grid=(m // tm,),
        in_specs=[row, row, pl.BlockSpec((d, d), lambda i: (0, 0))],
        out_specs=row,
        out_shape=jax.ShapeDtypeStruct((m, d), F32),
        compiler_params=_params("parallel"),
        name="matmul_residual",
    )(x, y, w_bf16)


def _swiglu_chunk(h_ref, wg_ref, wu_ref, wd_ref, acc_ref):
    h = h_ref[...]
    tf = wg_ref.shape[-1]
    piece = MXU_WIDTH if tf % MXU_WIDTH == 0 else tf
    cols = [slice(p, p + piece) for p in range(0, tf, piece)]
    gates = [_dot(h, wg_ref[0, :, c]) for c in cols]
    ups = [_dot(h, wu_ref[0, :, c]) for c in cols]
    acts = [(g * jax.nn.sigmoid(g) * u).astype(BF16) for g, u in zip(gates, ups)]
    downs = [_dot(a, wd_ref[0, c, :]) for a, c in zip(acts, cols)]
    acc_ref[...] += functools.reduce(lambda x, y: x + y, downs)


def _ffn_kernel(x_ref, g_ref, wg_ref, wu_ref, wd_ref, o_ref, h_ref, acc_ref):
    j = pl.program_id(1)

    @pl.when(j == 0)
    def _():
        h_ref[...] = _rms(x_ref[...], g_ref[...]).astype(BF16)
        acc_ref[...] = jnp.zeros_like(acc_ref)

    _swiglu_chunk(h_ref, wg_ref, wu_ref, wd_ref, acc_ref)

    @pl.when(j == pl.num_programs(1) - 1)
    def _():
        o_ref[...] = x_ref[...] + acc_ref[...]


def dense_ffn(x, g, wg, wu, wd, *, tm=1024, tf=512):
    m, d = x.shape
    f = wg.shape[1]
    tm, tf = _row_tile(m, tm), _row_tile(f, tf)
    return pl.pallas_call(
        _ffn_kernel,
        grid=(m // tm, f // tf),
        in_specs=[pl.BlockSpec((tm, d), lambda i, j: (i, 0)),
                  pl.BlockSpec((1, d), lambda i, j: (0, 0)),
                  pl.BlockSpec((1, d, tf), lambda i, j: (0, 0, j)),
                  pl.BlockSpec((1, d, tf), lambda i, j: (0, 0, j)),
                  pl.BlockSpec((1, tf, d), lambda i, j: (0, j, 0))],
        out_specs=pl.BlockSpec((tm, d), lambda i, j: (i, 0)),
        scratch_shapes=[pltpu.VMEM((tm, d), BF16), pltpu.VMEM((tm, d), F32)],
        out_shape=jax.ShapeDtypeStruct((m, d), F32),
        compiler_params=_params("parallel", "arbitrary"),
        name="dense_ffn",
    )(x, g.reshape(1, d), wg[None], wu[None], wd[None])


MOE_ROW_COPIES_PER_STEP = 128


def _moe_ffn_kernel(te_ref, first_ref, src_ref, x_hbm, wg_ref, wu_ref, wd_ref, o_ref,
                    xbuf_ref, sem_ref, wg_res_ref, wu_res_ref, wd_res_ref, h_ref, acc_ref, *, tm, copy_steps):
    i, j = pl.program_id(0), pl.program_id(1)
    slot = lax.rem(i, 2)
    rows_per_step = tm // copy_steps
    last_step = jnp.logical_and(i == pl.num_programs(0) - 1, j == pl.num_programs(1) - 1)

    def start_rows(tile, first_row):
        tile_slot = lax.rem(tile, 2)
        for r in range(rows_per_step):
            row = first_row + r
            token = src_ref[tile * tm + row]
            pltpu.make_async_copy(x_hbm.at[pl.ds(token, 1)], xbuf_ref.at[tile_slot, pl.ds(row, 1)],
                                  sem_ref.at[tile_slot]).start()

    def wait_half(half):
        pltpu.make_async_copy(x_hbm.at[pl.ds(0, tm)], xbuf_ref.at[half], sem_ref.at[half]).wait()

    @pl.when(jnp.logical_and(i == 0, j == 0))
    def _():
        def body(s, carry):
            start_rows(0, s * rows_per_step)
            return carry
        lax.fori_loop(0, copy_steps, body, 0)

    @pl.when(first_ref[i] > 0)
    def _():
        wg_res_ref[j] = wg_ref[0].astype(BF16)
        wu_res_ref[j] = wu_ref[0].astype(BF16)
        wd_res_ref[j] = wd_ref[0].astype(BF16)

    @pl.when(j == 0)
    def _():
        wait_half(slot)
        h_ref[...] = xbuf_ref[slot].astype(BF16)
        acc_ref[...] = jnp.zeros_like(acc_ref)

    @pl.when(j < copy_steps)
    def _():
        start_rows(i + 1, j * rows_per_step)

    _swiglu_chunk(h_ref, wg_res_ref.at[pl.ds(j, 1)], wu_res_ref.at[pl.ds(j, 1)], wd_res_ref.at[pl.ds(j, 1)], acc_ref)

    @pl.when(j == pl.num_programs(1) - 1)
    def _():
        o_ref[...] = acc_ref[...]

    @pl.when(last_step)
    def _():
        wait_half(1 - slot)


def moe_ffn(x, src_rows, wg, wu, wd, tile_expert, *, tm, tf=512):
    n_rows = src_rows.shape[0]
    d, f = wg.shape[1:]
    tf = _row_tile(f, tf)
    chunks = f // tf
    assert n_rows % tm == 0 and tf % LANES == 0 and x.shape[0] >= tm
    copy_steps = min(chunks, tm // MOE_ROW_COPIES_PER_STEP)
    assert tm % copy_steps == 0
    n_tiles = n_rows // tm
    changed = tile_expert[1:] != tile_expert[:-1]
    tile_first = jnp.concatenate([jnp.ones((1,), jnp.int32), changed.astype(jnp.int32)])
    src_padded = jnp.concatenate([src_rows, jnp.zeros((tm,), src_rows.dtype)])
    chunk = lambda i, j, first: jnp.where(first[i] > 0, j, chunks - 1)
    return pl.pallas_call(
        functools.partial(_moe_ffn_kernel, tm=tm, copy_steps=copy_steps),
        grid_spec=pltpu.PrefetchScalarGridSpec(
            num_scalar_prefetch=3,
            grid=(n_tiles, chunks),
            in_specs=[pl.BlockSpec(memory_space=pl.ANY),
                      pl.BlockSpec((1, d, tf), lambda i, j, te, first, src: (te[i], 0, chunk(i, j, first))),
                      pl.BlockSpec((1, d, tf), lambda i, j, te, first, src: (te[i], 0, chunk(i, j, first))),
                      pl.BlockSpec((1, tf, d), lambda i, j, te, first, src: (te[i], chunk(i, j, first), 0))],
            out_specs=pl.BlockSpec((tm, d), lambda i, j, te, first, src: (i, 0)),
            scratch_shapes=[pltpu.VMEM((2, tm, d), F32), pltpu.SemaphoreType.DMA((2,)),
                            pltpu.VMEM((chunks, d, tf), BF16), pltpu.VMEM((chunks, d, tf), BF16),
                            pltpu.VMEM((chunks, tf, d), BF16),
                            pltpu.VMEM((tm, d), BF16), pltpu.VMEM((tm, d), F32)]),
        out_shape=jax.ShapeDtypeStruct((n_rows, d), F32),
        compiler_params=_params("arbitrary", "arbitrary"),
        name="moe_ffn",
    )(tile_expert, tile_first, src_padded, x, wg, wu, wd)


def _router_kernel(x_ref, g_ref, wr_ref, h_ref, gates_ref, sel_ref, *, n_e):
    h = _rms(x_ref[...], g_ref[...])
    h_ref[...] = h
    logits = jnp.dot(h, wr_ref[...], preferred_element_type=F32, precision=lax.Precision.HIGHEST)
    idx = lax.broadcasted_iota(jnp.int32, logits.shape, 1).astype(F32)
    logits = jnp.where(idx < n_e, logits, -jnp.inf)
    m1 = jnp.max(logits, axis=-1, keepdims=True)
    i1 = jnp.min(jnp.where(logits == m1, idx, float(n_e)), axis=-1, keepdims=True)
    rest = jnp.where(idx == i1, -jnp.inf, logits)
    m2 = jnp.max(rest, axis=-1, keepdims=True)
    i2 = jnp.min(jnp.where(rest == m2, idx, float(n_e)), axis=-1, keepdims=True)
    e2 = jnp.exp(m2 - m1)
    g1 = 1.0 / (1.0 + e2)
    g2 = e2 / (1.0 + e2)
    gates_ref[...] = jnp.where(idx == i1, g1, jnp.where(idx == i2, g2, 0.0))
    sel_ref[...] = jnp.where(idx == i1, 1, jnp.where(idx == i2, 1, 0)).astype(jnp.int32)


def router(x, g, w_router, *, tm=512):
    m, d = x.shape
    n_e = w_router.shape[1]
    assert TOP_K <= n_e <= LANES
    tm = _row_tile(m, tm)
    wr = jnp.pad(w_router, ((0, 0), (0, LANES - n_e)))
    h, gates, sel = pl.pallas_call(
        functools.partial(_router_kernel, n_e=n_e),
        grid=(m // tm,),
        in_specs=[pl.BlockSpec((tm, d), lambda i: (i, 0)),
                  pl.BlockSpec((1, d), lambda i: (0, 0)),
                  pl.BlockSpec((d, LANES), lambda i: (0, 0))],
        out_specs=[pl.BlockSpec((tm, d), lambda i: (i, 0)),
                   pl.BlockSpec((tm, LANES), lambda i: (i, 0)),
                   pl.BlockSpec((tm, LANES), lambda i: (i, 0))],
        out_shape=[jax.ShapeDtypeStruct((m, d), F32),
                   jax.ShapeDtypeStruct((m, LANES), F32),
                   jax.ShapeDtypeStruct((m, LANES), jnp.int32)],
        compiler_params=_params("parallel"),
        name="router",
    )(x, g.reshape(1, d), wr)
    return h, gates[:, :n_e], sel[:, :n_e]


def _combine_kernel(x_ref, y1_ref, y2_ref, gt_ref, g_ref, o_ref, *, normed):
    gt = gt_ref[...]
    y = x_ref[...] + gt[:, 0:1] * y1_ref[...] + gt[:, 1:2] * y2_ref[...]
    o_ref[...] = _rms(y, g_ref[...]) if normed else y


def moe_combine(x, y1, y2, gates2, g_out, *, tm=512):
    m, d = x.shape
    tm = _row_tile(m, tm)
    spec = pl.BlockSpec((tm, d), lambda i: (i, 0))
    normed = g_out is not None
    g_row = (g_out if normed else jnp.ones((d,), F32)).reshape(1, d)
    return pl.pallas_call(
        functools.partial(_combine_kernel, normed=normed),
        grid=(m // tm,),
        in_specs=[spec, spec, spec, pl.BlockSpec((tm, TOP_K), lambda i: (i, 0)),
                  pl.BlockSpec((1, d), lambda i: (0, 0))],
        out_specs=spec,
        out_shape=jax.ShapeDtypeStruct((m, d), F32),
        compiler_params=_params("parallel"),
        name="moe_combine",
    )(x, y1, y2, gates2, g_row)


MOE_TILE = 512


def moe_layer(xs, g, w_router, wg, wu, wd, g_out=None):
    n_e = w_router.shape[1]
    tm = MOE_TILE
    routed = [router(x, g, w_router) for x in xs]
    h, gates8, sel8 = (jnp.concatenate(parts, axis=0) for parts in zip(*routed))
    m = h.shape[0]
    lanes = jnp.arange(n_e, dtype=jnp.int32)
    ids = jnp.stack([jnp.min(jnp.where(sel8 > 0, lanes, n_e), axis=-1),
                     jnp.max(jnp.where(sel8 > 0, lanes, -1), axis=-1)], axis=-1)
    pick = lambda table: jnp.stack([jnp.sum(jnp.where(lanes == ids[:, k:k + 1], table, 0), axis=-1)
                                    for k in range(TOP_K)], axis=-1)
    gates2 = pick(gates8)
    n_tiles = (m * TOP_K + n_e * (tm - 1)) // tm
    rank = jnp.cumsum(sel8, axis=0) - sel8
    counts = rank[-1] + sel8[-1]
    padded = ((counts + tm - 1) // tm) * tm
    pad_end = jnp.cumsum(padded)
    pad_start = pad_end - padded
    pos2 = pick(pad_start[None, :] + rank)
    token = jnp.broadcast_to(jnp.arange(m, dtype=jnp.int32)[:, None], (m, TOP_K))
    src_token = jnp.zeros((n_tiles * tm,), jnp.int32).at[pos2.reshape(-1)].set(token.reshape(-1))
    tile_start = jnp.arange(n_tiles, dtype=jnp.int32) * tm
    tile_expert = jnp.minimum(jnp.sum(tile_start[:, None] >= pad_end[None, :], axis=-1), n_e - 1).astype(jnp.int32)
    y_sorted = moe_ffn(h, src_token, wg, wu, wd, tile_expert, tm=tm)
    results, row0 = [], 0
    for x in xs:
        rows = slice(row0, row0 + x.shape[0])
        y1 = y_sorted.at[pos2[rows, 0]].get(mode="promise_in_bounds")
        y2 = y_sorted.at[pos2[rows, 1]].get(mode="promise_in_bounds")
        results.append(moe_combine(x, y1, y2, gates2[rows], g_out))
        row0 += x.shape[0]
    return results


def _final_norm_kernel(x_ref, g_ref, o_ref):
    o_ref[...] = _rms(x_ref[...], g_ref[...])


def final_norm(x, g, *, tm=512):
    m, d = x.shape
    tm = _row_tile(m, tm)
    return pl.pallas_call(
        _final_norm_kernel,
        grid=(m // tm,),
        in_specs=[pl.BlockSpec((tm, d), lambda i: (i, 0)), pl.BlockSpec((1, d), lambda i: (0, 0))],
        out_specs=pl.BlockSpec((tm, d), lambda i: (i, 0)),
        out_shape=jax.ShapeDtypeStruct((m, d), F32),
        compiler_params=_params("parallel"),
        name="final_norm",
    )(x, g.reshape(1, d))


def _block_diag(w):
    g, c, _ = w.shape
    eye = jnp.eye(g, dtype=w.dtype)
    return (eye[:, None, :, None] * w[:, :, None, :]).reshape(g * c, g * c)


def _pad_time(a, t):
    return jnp.pad(a, ((0, 0), (0, t - a.shape[1]), (0, 0)))


def kernel(x_prompt, x_sample, cache_k, cache_v, cache_mem_k, cache_mem_v, state_pool, state_sconv, state_conf, page_table, mem_prompt, g_mix, w_in, w_pool, pool_scale, w_sconv, sb_bias, w_conf, b_conf, g_conf_ln, b_conf_ln, w_out, g_cross, g_mem, w_mq, w_mk, w_mv, w_mo, g_ffn, w_ff_gate, w_ff_up, w_ff_down, w_router, w_moe_gate, w_moe_up, w_moe_down, g_final):
    bp, tp, d = x_prompt.shape
    bs, ts, _ = x_sample.shape
    depth = g_mix.shape[0]
    c = d // 4
    dh = c // SB_HEADS
    page = cache_k.shape[2]
    past = page_table.shape[1] * page
    n_mem = mem_prompt.shape[1]
    qscale = float(dh ** -0.5)
    assert ts <= SAMPLE_ROWS and ts <= page

    bf = lambda a: a.astype(BF16)
    w_in_b, w_out_b, w_mq_b, w_mk_b, w_mv_b, w_mo_b = map(bf, (w_in, w_out, w_mq, w_mk, w_mv, w_mo))
    cache_kt = cache_k.transpose(0, 1, 3, 4, 2).reshape(depth, -1, c, page)
    cache_vt = cache_v.transpose(0, 1, 3, 4, 2).reshape(depth, -1, c, page)
    head_mask = (jnp.arange(c)[None, :] // dh == jnp.arange(SB_HEADS)[:, None]).astype(F32)
    mem_k_all = cache_mem_k.reshape(depth, bs, cache_mem_k.shape[2], d)
    mem_v_all = cache_mem_v.reshape(depth, bs, cache_mem_v.shape[2], d)

    xp = x_prompt.reshape(bp * tp, d)
    xs = x_sample.reshape(bs * ts, d)
    zeros_state = lambda keep: jnp.zeros((bp, keep, c), F32)
    outs = {name: [] for name in ("kp", "vp", "ks", "vs", "mkp", "mvp", "poolp", "pools", "scp", "scs", "cfp", "cfs")}

    for i in range(depth):
        wpool_bd = bf(_block_diag(w_pool[i]))
        mixer_w = (wpool_bd, pool_scale[i], w_sconv[i], w_conf[i], b_conf[i], g_conf_ln[i], b_conf_ln[i])

        pa, pb, q, k_t, v_t, kb, vb, pd = in_proj(xp, g_mix[i], w_in_b[i], qscale=qscale, seq_len=tp)
        r3 = lambda a: a.reshape(bp, tp, a.shape[-1])
        ya, yb, yd, n_pool, n_sc, n_cf = local_mixers(
            r3(pa), r3(pb), r3(pd), zeros_state(max(POOL_WINDOWS) - 1), zeros_state(w_sconv.shape[1] - 1),
            zeros_state(w_conf.shape[1] - 1), *mixer_w, t_valid=tp, pos0=0)
        yc = sb_prompt(r3(q), r3(kb), r3(vb), sb_bias[i])
        f2 = lambda a: a.reshape(bp * tp, c)
        xp = out_proj(xp, f2(ya), f2(yb), f2(yc), f2(yd), w_out_b[i])
        heads_last = lambda a_t: a_t.reshape(bp, SB_HEADS, dh, tp).transpose(0, 3, 1, 2)
        outs["kp"].append(heads_last(k_t))
        outs["vp"].append(heads_last(v_t))
        outs["poolp"].append(n_pool); outs["scp"].append(n_sc); outs["cfp"].append(n_cf)

        mem2 = mem_prompt.reshape(bp * n_mem, d)
        mk_p = norm_matmul(mem2, g_mem[i], w_mk_b[i]).reshape(bp, n_mem, d)
        mv_p = norm_matmul(mem2, g_mem[i], w_mv_b[i]).reshape(bp, n_mem, d)
        outs["mkp"].append(mk_p.reshape(bp, n_mem, MEM_HEADS, d // MEM_HEADS))
        outs["mvp"].append(mv_p.reshape(bp, n_mem, MEM_HEADS, d // MEM_HEADS))
        xp = cross_block(xp.reshape(bp, tp, d), g_cross[i], w_mq_b[i], mk_p, mv_p, w_mo_b[i]).reshape(bp * tp, d)

        pa, pb, q, k, v, kb, vb, pd = in_proj(xs, g_mix[i], w_in_b[i], qscale=qscale)
        r3 = lambda a: _pad_time(a.reshape(bs, ts, a.shape[-1]), SAMPLE_ROWS)
        ya, yb, yd, n_pool, n_sc, n_cf = local_mixers(
            r3(pa), r3(pb), r3(pd), state_pool[i], state_sconv[i], state_conf[i], *mixer_w,
            t_valid=ts, pos0=past)
        q_s = _pad_time(q.astype(F32).reshape(bs, ts, c), SAMPLE_ROWS)
        q_bd = (q_s[:, None] * head_mask[None, :, None]).reshape(bs, SB_HEADS * SAMPLE_ROWS, c)
        new_t = lambda a: jnp.pad(a.reshape(bs, ts, c).transpose(0, 2, 1), ((0, 0), (0, 0), (0, page - ts)))
        yc = sb_sample(q_bd, new_t(k), new_t(v), cache_kt, cache_vt, i, page_table, sb_bias[i])
        yc = yc[:, :ts].reshape(bs * ts, c)
        f2 = lambda a: a[:, :ts].reshape(bs * ts, c)
        xs = out_proj(xs, f2(ya), f2(yb), yc, f2(yd), w_out_b[i])
        outs["ks"].append(k.reshape(bs, ts, SB_HEADS, dh))
        outs["vs"].append(v.reshape(bs, ts, SB_HEADS, dh))
        outs["pools"].append(n_pool); outs["scs"].append(n_sc); outs["cfs"].append(n_cf)

        q_mem = _pad_time(norm_matmul(xs, g_cross[i], w_mq_b[i]).reshape(bs, ts, d), 2 * SUBLANES)
        o_mem = mem_attend(q_mem, mem_k_all, mem_v_all, i)
        xs = matmul_residual(xs, o_mem[:, :ts].reshape(bs * ts, d), w_mo_b[i])

        j = i // 2
        if i % 2 == 0:
            wg, wu, wd = bf(w_ff_gate[j]), bf(w_ff_up[j]), bf(w_ff_down[j])
            xp = dense_ffn(xp, g_ffn[i], wg, wu, wd)
            xs = dense_ffn(xs, g_ffn[i], wg, wu, wd)
        else:
            xp, xs = moe_layer([xp, xs], g_ffn[i], w_router[j], w_moe_gate[j], w_moe_up[j], w_moe_down[j],
                               g_out=g_final if i == depth - 1 else None)

    if depth % 2 == 1:
        xp, xs = final_norm(xp, g_final), final_norm(xs, g_final)
    y_prompt = xp.reshape(bp, tp, d)
    y_sample = xs.reshape(bs, ts, d)
    st = lambda name: jnp.stack(outs[name])
    return (y_prompt, y_sample, st("kp"), st("vp"), st("ks"), st("vs"), st("mkp"), st("mvp"),
            st("poolp"), st("pools"), st("scp"), st("scs"), st("cfp"), st("cfs"))
```
